```python
import math
import jax, jax.numpy as jnp
from jax import lax
import numpy as np

D_MODEL = 1024
BATCH = 4
SEQ = 8192
DEPTH = 4

N_MEM = 256
D_MIX = D_MODEL
N_MIXERS = 4
W_GROUP = D_MIX // N_MIXERS
EPS = 1e-6

S5_GROUP_CH = 16
S5_GROUPS = W_GROUP // S5_GROUP_CH
S5_STATE = 64
S5_DT_MIN = 1e-3
S5_DT_MAX = 1e-1

GLA_HEADS = 4
GLA_DV = W_GROUP // GLA_HEADS
GLA_DK = GLA_DV // 2
GLA_RANK = 16
GLA_TAU = 16.0
GLA_CHUNK = 16

RW_HEADS = 4
RW_N = W_GROUP // RW_HEADS
RW_W_RANK = 32
RW_A_RANK = 32
RW_G_RANK = 64
RW_LN_EPS = 64e-5

CONV_WIDTH = 31
CONV_LN_EPS = 1e-5

X_HEADS = 4
X_HEAD_DIM = D_MODEL // X_HEADS

N_GROUPS = 4
EXP_PER_GROUP = 8
N_EXPERTS = N_GROUPS * EXP_PER_GROUP
TOP_K = 2
D_EXPERT = 512
MOE_BLOCK = 256

S5_COLS = W_GROUP
GLA_COLS = 2 * GLA_HEADS * GLA_DK + 2 * W_GROUP + GLA_RANK
RW_COLS = 3 * W_GROUP + RW_W_RANK + RW_A_RANK + RW_G_RANK
CONV_COLS = 2 * W_GROUP
IN_COLS = S5_COLS + GLA_COLS + RW_COLS + CONV_COLS
COL_SPLITS = (S5_COLS, S5_COLS + GLA_COLS, S5_COLS + GLA_COLS + RW_COLS)

kernel_name = 'hybrid_parallel_headgroup_hmoe_block'


def rms_norm(x, g):
    xf = x.astype(jnp.float32)
    y = xf * lax.rsqrt(jnp.mean(xf * xf, axis=-1, keepdims=True) + EPS)
    return (y * g.astype(jnp.float32)).astype(x.dtype)


def standardize(x, eps):
    mu = jnp.mean(x, axis=-1, keepdims=True)
    xc = x - mu
    return xc * lax.rsqrt(jnp.mean(xc * xc, axis=-1, keepdims=True) + eps)


def layer_norm(x, g, b, eps):
    return standardize(x, eps) * g + b


def s5_mixer(u, lam_re, lam_im, b_re, b_im, c_re, c_im, d_skip, log_dt, glu_w, glu_b):
    f32 = jnp.float32
    bsz, seq, _ = u.shape
    u = u.astype(f32)
    lam = lax.complex(lam_re.astype(f32), lam_im.astype(f32))
    dt = jnp.exp(log_dt.astype(f32))[:, None]
    lam_bar = jnp.exp(lam * dt)
    b_bar = ((lam_bar - 1.0) / lam)[:, :, None] * lax.complex(b_re.astype(f32), b_im.astype(f32))
    ug = u.reshape(bsz, seq, S5_GROUPS, S5_GROUP_CH).astype(jnp.complex64)
    bu = jnp.einsum('gnp,blgp->blgn', b_bar, ug)
    decay = jnp.broadcast_to(lam_bar, bu.shape)

    def combine(left, right):
        a_l, b_l = left
        a_r, b_r = right
        return a_l * a_r, a_r * b_l + b_r

    _, states = lax.associative_scan(combine, (decay, bu), axis=1)
    c = lax.complex(c_re.astype(f32), c_im.astype(f32))
    y = jnp.einsum('gpn,blgn->blgp', c, states).real.reshape(bsz, seq, W_GROUP)
    y = jax.nn.gelu(y + d_skip * u)
    return y * jax.nn.sigmoid(y @ glu_w + glu_b)


def gla_mixer(p, w_up, b_up, norm_g):
    f32 = jnp.float32
    bsz, seq, _ = p.shape
    H, dk, dv, C = GLA_HEADS, GLA_DK, GLA_DV, GLA_CHUNK
    nc = seq // C
    q, k, v, g, z = jnp.split(p.astype(f32), [H * dk, 2 * H * dk, 2 * H * dk + W_GROUP, 2 * H * dk + 2 * W_GROUP], axis=-1)
    log_alpha = jax.nn.log_sigmoid(z @ w_up + b_up) / GLA_TAU

    def chunks(t, d):
        return t.reshape(bsz, nc, C, H, d).transpose(0, 3, 1, 2, 4)

    q = chunks(q * dk ** -0.5, dk)
    k = chunks(k, dk)
    v = chunks(v, dv)
    bcum = jnp.cumsum(chunks(log_alpha, dk), axis=3)
    causal = jnp.tril(jnp.ones((C, C), dtype=bool))[:, :, None]
    rel = jnp.where(causal, bcum[:, :, :, :, None, :] - bcum[:, :, :, None, :, :], -jnp.inf)
    scores = jnp.sum(q[:, :, :, :, None, :] * k[:, :, :, None, :, :] * jnp.exp(rel), axis=-1)
    o_intra = jnp.einsum('bhnij,bhnjv->bhniv', scores, v)
    b_last = bcum[:, :, :, -1:, :]
    chunk_upd = jnp.einsum('bhncd,bhncv->bhndv', k * jnp.exp(b_last - bcum), v)
    chunk_decay = jnp.exp(b_last[:, :, :, 0, :])

    def carry_state(state, inp):
        dec, upd = inp
        return dec[..., None] * state + upd, state

    s0 = jnp.zeros((bsz, H, dk, dv), f32)
    _, s_prev = lax.scan(carry_state, s0, (chunk_decay.transpose(2, 0, 1, 3), chunk_upd.transpose(2, 0, 1, 3, 4)))
    o_inter = jnp.einsum('bhncd,bhndv->bhncv', q * jnp.exp(bcum), s_prev.transpose(1, 2, 0, 3, 4))
    o = (o_intra + o_inter).transpose(0, 2, 3, 1, 4).reshape(bsz, seq, H, dv)
    o = o * lax.rsqrt(jnp.mean(o * o, axis=-1, keepdims=True) + EPS)
    return o.reshape(bsz, seq, W_GROUP) * norm_g * jax.nn.silu(g)


def rwkv7_mixer(p, mu, w0, w2, a0, a2, g2, k_k, k_a, r_k, ln_g, ln_b):
    f32 = jnp.float32
    bsz, seq, _ = p.shape
    p = p.astype(f32)
    p_prev = jnp.pad(p, ((0, 0), (1, 0), (0, 0)))[:, :-1]
    p = p + (p_prev - p) * mu
    r, k, v, zw, za, zg = jnp.split(p, [W_GROUP, 2 * W_GROUP, 3 * W_GROUP, 3 * W_GROUP + RW_W_RANK, 3 * W_GROUP + RW_W_RANK + RW_A_RANK], axis=-1)
    w = -jax.nn.softplus(-(w0 + jnp.tanh(zw) @ w2)) - 0.5
    decay = jnp.exp(-jnp.exp(w))
    a = jax.nn.sigmoid(a0 + za @ a2)
    g = jax.nn.sigmoid(zg) @ g2

    def heads(t):
        return t.reshape(bsz, seq, RW_HEADS, RW_N)

    kk = heads(k * k_k)
    kk = kk / jnp.maximum(jnp.sqrt(jnp.sum(kk * kk, axis=-1, keepdims=True)), 1e-12)
    k = k * (1.0 + (a - 1.0) * k_a)
    r_h, k_h, v_h = heads(r), heads(k), heads(v)

    def time_step(state, inp):
        r_t, w_t, k_t, v_t, a_t, b_t = inp
        sa = jnp.einsum('bhvk,bhk->bhv', state, a_t)
        state = state * w_t[:, :, None, :] + sa[..., None] * b_t[:, :, None, :] + v_t[..., None] * k_t[:, :, None, :]
        return state, jnp.einsum('bhvk,bhk->bhv', state, r_t)

    xs = tuple(t.transpose(1, 0, 2, 3) for t in (r_h, heads(decay), k_h, v_h, -kk, kk * heads(a)))
    s0 = jnp.zeros((bsz, RW_HEADS, RW_N, RW_N), f32)
    _, y = lax.scan(time_step, s0, xs)
    y = standardize(y.transpose(1, 0, 2, 3), RW_LN_EPS)
    y = y.reshape(bsz, seq, W_GROUP) * ln_g + ln_b
    bonus = jnp.sum(r_h * k_h * r_k, axis=-1, keepdims=True) * v_h
    return (y + bonus.reshape(bsz, seq, W_GROUP)) * g


def conv_mixer(p, dw_w, dw_b, ln_g, ln_b):
    pf = p.astype(jnp.float32)
    u = pf[..., :W_GROUP] * jax.nn.sigmoid(pf[..., W_GROUP:])
    y = lax.conv_general_dilated(u, dw_w.astype(jnp.float32)[:, None, :], window_strides=(1,),
                                 padding=[(CONV_WIDTH - 1, 0)], dimension_numbers=('NWC', 'WIO', 'NWC'),
                                 feature_group_count=W_GROUP)
    y = layer_norm(y + dw_b, ln_g, ln_b, CONV_LN_EPS)
    return jax.nn.silu(y)


def memory_xattn(hn, mn, wq, wk, wv, wo):
    bsz, seq, d = hn.shape
    n_mem = mn.shape[1]
    q = (hn @ wq).reshape(bsz, seq, X_HEADS, X_HEAD_DIM)
    k = (mn @ wk).reshape(bsz, n_mem, X_HEADS, X_HEAD_DIM)
    v = (mn @ wv).reshape(bsz, n_mem, X_HEADS, X_HEAD_DIM)
    s = jnp.einsum('blhd,bmhd->bhlm', q, k).astype(jnp.float32) * X_HEAD_DIM ** -0.5
    pr = jax.nn.softmax(s, axis=-1).astype(v.dtype)
    o = jnp.einsum('bhlm,bmhd->blhd', pr, v).reshape(bsz, seq, d)
    return (o @ wo).astype(hn.dtype)


def hier_moe(xn, group_w, group_b, expert_w, expert_b, w_gate, w_up, w_down):
    f32 = jnp.float32
    bsz, seq, d = xn.shape
    T = bsz * seq
    xt = xn.reshape(T, d)
    g_logits = (xt @ group_w).astype(f32) + group_b
    g_sel = jnp.argmax(g_logits, axis=-1)
    g_w = jnp.take_along_axis(jax.nn.softmax(g_logits, axis=-1), g_sel[:, None], axis=-1)
    e_logits = ((xt @ expert_w).astype(f32) + expert_b).reshape(T, N_GROUPS, EXP_PER_GROUP)
    e_in_group = jnp.take_along_axis(e_logits, g_sel[:, None, None], axis=1)[:, 0]
    top_v, top_i = lax.top_k(e_in_group, TOP_K)
    gate = (jax.nn.softmax(top_v, axis=-1) * g_w).reshape(-1)
    expert_id = (g_sel[:, None] * EXP_PER_GROUP + top_i).reshape(-1).astype(jnp.int32)
    token_id = jnp.repeat(jnp.arange(T, dtype=jnp.int32), TOP_K)
    order = jnp.argsort(expert_id)
    e_sorted = expert_id[order]
    counts = jnp.bincount(expert_id, length=N_EXPERTS).astype(jnp.int32)
    starts = jnp.cumsum(counts) - counts
    padded = (counts + MOE_BLOCK - 1) // MOE_BLOCK * MOE_BLOCK
    pad_ends = jnp.cumsum(padded)
    pad_starts = pad_ends - padded
    dest = pad_starts[e_sorted] + (jnp.arange(T * TOP_K, dtype=jnp.int32) - starts[e_sorted])
    n_slots = ((T * TOP_K + MOE_BLOCK - 1) // MOE_BLOCK + N_EXPERTS) * MOE_BLOCK
    n_blocks = n_slots // MOE_BLOCK
    slot_tok = jnp.full((n_slots,), T, jnp.int32).at[dest].set(token_id[order])
    slot_gate = jnp.zeros((n_slots,), f32).at[dest].set(gate[order])
    block_start = jnp.arange(n_blocks, dtype=jnp.int32) * MOE_BLOCK
    block_exp = jnp.minimum(jnp.searchsorted(pad_ends, block_start, side='right'), N_EXPERTS - 1)
    x_pad = jnp.concatenate([xt, jnp.zeros((1, d), xt.dtype)], axis=0)
    xb = x_pad[slot_tok].reshape(n_blocks, MOE_BLOCK, d)

    def expert_block(args):
        xblk, e = args
        hid = jax.nn.silu(xblk @ w_gate[e]) * (xblk @ w_up[e])
        return hid @ w_down[e]

    yb = lax.map(expert_block, (xb, block_exp)).reshape(n_slots, d).astype(f32)
    y = jnp.zeros((T + 1, d), f32).at[slot_tok].add(yb * slot_gate[:, None])
    return y[:T].reshape(bsz, seq, d).astype(xn.dtype)


def setup_inputs(seed: int = 0) -> dict:
    f32 = jnp.float32
    key = jax.random.key(seed)
    keys = jax.random.split(key, 64)
    count = [0]

    def nk():
        count[0] += 1
        return keys[count[0] - 1]

    def nrm(shape, scale):
        return jax.random.normal(nk(), shape, f32) * scale

    def unif(shape, lo, hi):
        return jax.random.uniform(nk(), shape, f32, lo, hi)

    def gain(shape):
        return 1.0 + nrm(shape, 0.02)

    L = DEPTH
    D = D_MODEL
    G, N, P = S5_GROUPS, S5_STATE, S5_GROUP_CH
    res_scale = (2.0 * DEPTH) ** -0.5
    return {
        'x': nrm((BATCH, SEQ, D), 1.0),
        'mem': nrm((BATCH, N_MEM, D), 1.0),
        'norm_mix_g': gain((L, D)),
        'w_in': nrm((L, D, IN_COLS), D ** -0.5),
        'w_out': nrm((L, D_MIX, D), D_MIX ** -0.5 * res_scale),
        'mix_beta': gain((L, D_MIX)),
        's5_lam_re': -0.5 + nrm((L, G, N), 0.01),
        's5_lam_im': jnp.pi * jnp.arange(N, dtype=f32) + nrm((L, G, N), 0.01),
        's5_b_re': nrm((L, G, N, P), (2.0 * P) ** -0.5),
        's5_b_im': nrm((L, G, N, P), (2.0 * P) ** -0.5),
        's5_c_re': nrm((L, G, P, N), N ** -0.5),
        's5_c_im': nrm((L, G, P, N), N ** -0.5),
        's5_d': nrm((L, W_GROUP), 1.0),
        's5_log_dt': unif((L, G), math.log(S5_DT_MIN), math.log(S5_DT_MAX)),
        's5_glu_w': nrm((L, W_GROUP, W_GROUP), W_GROUP ** -0.5),
        's5_glu_b': nrm((L, W_GROUP), 0.01),
        'gla_w_up': nrm((L, GLA_RANK, GLA_HEADS * GLA_DK), GLA_RANK ** -0.5),
        'gla_b_up': unif((L, GLA_HEADS * GLA_DK), 0.0, 4.0),
        'gla_norm_g': gain((L, W_GROUP)),
        'rw_mu': unif((L, RW_COLS), 0.0, 1.0),
        'rw_w0': unif((L, W_GROUP), -6.0, -1.0),
        'rw_w2': nrm((L, RW_W_RANK, W_GROUP), 0.1),
        'rw_a0': nrm((L, W_GROUP), 0.1),
        'rw_a2': nrm((L, RW_A_RANK, W_GROUP), 0.1),
        'rw_g2': nrm((L, RW_G_RANK, W_GROUP), RW_G_RANK ** -0.5),
        'rw_k_k': 0.85 + nrm((L, W_GROUP), 0.02),
        'rw_k_a': gain((L, W_GROUP)),
        'rw_r_k': nrm((L, RW_HEADS, RW_N), 0.1),
        'rw_ln_g': gain((L, W_GROUP)),
        'rw_ln_b': nrm((L, W_GROUP), 0.01),
        'conv_w': nrm((L, CONV_WIDTH, W_GROUP), CONV_WIDTH ** -0.5),
        'conv_b': nrm((L, W_GROUP), 0.01),
        'conv_ln_g': gain((L, W_GROUP)),
        'conv_ln_b': nrm((L, W_GROUP), 0.01),
        'norm_xattn_g': gain((L, D)),
        'norm_mem_g': gain((L, D)),
        'xa_wq': nrm((L, D, D), D ** -0.5),
        'xa_wk': nrm((L, D, D), D ** -0.5),
        'xa_wv': nrm((L, D, D), D ** -0.5),
        'xa_wo': nrm((L, D, D), D ** -0.5 * res_scale),
        'norm_ffn_g': gain((L, D)),
        'moe_group_w': nrm((L, D, N_GROUPS), D ** -0.5),
        'moe_group_b': nrm((L, N_GROUPS), 0.01),
        'moe_expert_w': nrm((L, D, N_EXPERTS), D ** -0.5),
        'moe_expert_b': nrm((L, N_EXPERTS), 0.01),
        'moe_w_gate': nrm((L, N_EXPERTS, D, D_EXPERT), D ** -0.5),
        'moe_w_up': nrm((L, N_EXPERTS, D, D_EXPERT), D ** -0.5),
        'moe_w_down': nrm((L, N_EXPERTS, D_EXPERT, D), D_EXPERT ** -0.5 * res_scale),
        'norm_final_g': gain((D,)),
    }


def reference(x, mem, norm_mix_g, w_in, w_out, mix_beta, s5_lam_re, s5_lam_im, s5_b_re, s5_b_im, s5_c_re, s5_c_im, s5_d, s5_log_dt, s5_glu_w, s5_glu_b, gla_w_up, gla_b_up, gla_norm_g, rw_mu, rw_w0, rw_w2, rw_a0, rw_a2, rw_g2, rw_k_k, rw_k_a, rw_r_k, rw_ln_g, rw_ln_b, conv_w, conv_b, conv_ln_g, conv_ln_b, norm_xattn_g, norm_mem_g, xa_wq, xa_wk, xa_wv, xa_wo, norm_ffn_g, moe_group_w, moe_group_b, moe_expert_w, moe_expert_b, moe_w_gate, moe_w_up, moe_w_down, norm_final_g):
    h = x
    for l in range(DEPTH):
        xn = rms_norm(h, norm_mix_g[l])
        p_s5, p_gla, p_rw, p_conv = jnp.split(xn @ w_in[l], COL_SPLITS, axis=-1)
        y_s5 = s5_mixer(p_s5, s5_lam_re[l], s5_lam_im[l], s5_b_re[l], s5_b_im[l], s5_c_re[l], s5_c_im[l],
                        s5_d[l], s5_log_dt[l], s5_glu_w[l], s5_glu_b[l])
        y_gla = gla_mixer(p_gla, gla_w_up[l], gla_b_up[l], gla_norm_g[l])
        y_rw = rwkv7_mixer(p_rw, rw_mu[l], rw_w0[l], rw_w2[l], rw_a0[l], rw_a2[l], rw_g2[l], rw_k_k[l],
                           rw_k_a[l], rw_r_k[l], rw_ln_g[l], rw_ln_b[l])
        y_conv = conv_mixer(p_conv, conv_w[l], conv_b[l], conv_ln_g[l], conv_ln_b[l])
        mixed = jnp.concatenate([y_s5, y_gla, y_rw, y_conv], axis=-1) * mix_beta[l]
        h = h + (mixed @ w_out[l]).astype(h.dtype)
        h = h + memory_xattn(rms_norm(h, norm_xattn_g[l]), rms_norm(mem, norm_mem_g[l]),
                             xa_wq[l], xa_wk[l], xa_wv[l], xa_wo[l])
        h = h + hier_moe(rms_norm(h, norm_ffn_g[l]), moe_group_w[l], moe_group_b[l], moe_expert_w[l],
                         moe_expert_b[l], moe_w_gate[l], moe_w_up[l], moe_w_down[l])
    return rms_norm(h, norm_final_g)
```

```python
import functools
import math

import jax
import jax.numpy as jnp
from jax import lax
from jax.experimental import pallas as pl
from jax.experimental.pallas import tpu as pltpu

F32 = jnp.float32
BF16 = jnp.bfloat16

D_MODEL = 1024
W_GROUP = 256
EPS = 1e-6

S5_GROUP_CH = 16
S5_GROUPS = 16
S5_STATE = 64
S5_NS = S5_GROUPS * S5_STATE

GLA_HEADS = 4
GLA_DV = 64
GLA_DK = 32
GLA_RANK = 16
GLA_TAU = 16.0
GLA_CHUNK = 16
GLA_COLS_PAD = 896

RW_HEADS = 4
RW_N = 64
RW_W_RANK = 32
RW_A_RANK = 32
RW_G_RANK = 64
RW_LN_EPS = 64e-5
RW_COLS = 896
RW_CHUNK = 64

CONV_WIDTH = 31
CONV_LN_EPS = 1e-5
CONV_HALO = 32

X_HEADS = 4
X_HEAD_DIM = 256

N_GROUPS = 4
EXP_PER_GROUP = 8
N_EXPERTS = 32
D_EXPERT = 512
MOE_BLOCK = 256
ROUTE_ROWS = 40
ROUTE_TILE = 512

VMEM_LIMIT = 56 * 1024 * 1024


def _cparams(*sem):
    return pltpu.CompilerParams(dimension_semantics=sem, vmem_limit_bytes=VMEM_LIMIT)


_NN = (((1,), (0,)), ((), ()))
_NT = (((1,), (1,)), ((), ()))
_TN = (((0,), (0,)), ((), ()))
_BNN = (((2,), (1,)), ((0,), (0,)))


def _dg(a, b, dims=_NN):
    return lax.dot_general(a, b, dims, preferred_element_type=F32)


def _split2(a):
    hi = a.astype(BF16)
    lo = (a - hi.astype(F32)).astype(BF16)
    return hi, lo


def _split3(a):
    hi = a.astype(BF16)
    r1 = a - hi.astype(F32)
    mid = r1.astype(BF16)
    lo = (r1 - mid.astype(F32)).astype(BF16)
    return hi, mid, lo


def _dot3(a, b, dims=_NN):
    ah, al = _split2(a)
    bh, bl = _split2(b)
    return _dg(ah, bh, dims) + (_dg(ah, bl, dims) + _dg(al, bh, dims))


def _dot_exact_rhs(a, b_bf16, dims=_NN):
    ah, am, al = _split3(a)
    return _dg(ah, b_bf16, dims) + (_dg(am, b_bf16, dims) + _dg(al, b_bf16, dims))


def _dot_exact_lhs(a_bf16, b, dims=_NN):
    bh, bm, bl = _split3(b)
    return _dg(a_bf16, bh, dims) + (_dg(a_bf16, bm, dims) + _dg(a_bf16, bl, dims))


def _iota(shape, dim):
    return lax.broadcasted_iota(jnp.int32, shape, dim)


def _rms(x, g):
    return x * lax.rsqrt(jnp.mean(x * x, axis=-1, keepdims=True) + EPS) * g


def _sigmoid(x):
    return 1.0 / (1.0 + jnp.exp(-x))


def _softplus(x):
    return jnp.maximum(x, 0.0) + jnp.log1p(jnp.exp(-jnp.abs(x)))


def _shift_rows(x, s, fill=None):
    rolled = pltpu.roll(x, s, 0)
    rows = _iota(x.shape, 0)
    if fill is None:
        fill = jnp.zeros_like(x)
    return jnp.where(rows >= s, rolled, fill)


def _in_proj_kernel(x_ref, g_ref, w5_ref, wg_ref, wr_ref, wc_ref, o5_ref, og_ref, or_ref, oc_ref):
    xb = _rms(x_ref[...], g_ref[...]).astype(BF16)
    o5_ref[...] = _dg(xb, w5_ref[...])
    og_ref[...] = _dg(xb, wg_ref[...])
    or_ref[...] = _dg(xb, wr_ref[...])
    oc_ref[...] = _dg(xb, wc_ref[...])


def _in_proj(h, g, w5, wg, wr, wc, tm=512):
    T = h.shape[0]
    ws = (w5, wg, wr, wc)
    row = lambda i: (i, 0)
    fixed = lambda i: (0, 0)
    return pl.pallas_call(
        _in_proj_kernel,
        grid=(T // tm,),
        in_specs=[pl.BlockSpec((tm, D_MODEL), row), pl.BlockSpec((1, D_MODEL), fixed)]
        + [pl.BlockSpec(w.shape, fixed) for w in ws],
        out_specs=[pl.BlockSpec((tm, w.shape[1]), row) for w in ws],
        out_shape=[jax.ShapeDtypeStruct((T, w.shape[1]), F32) for w in ws],
        compiler_params=_cparams("parallel"),
        name="in_proj",
    )(h, g, *ws)


def _kv_proj_kernel(m_ref, g_ref, wk_ref, wv_ref, k_ref, v_ref):
    mb = _rms(m_ref[...], g_ref[...]).astype(BF16)
    k_ref[...] = _dg(mb, wk_ref[...]).astype(BF16)
    v_ref[...] = _dg(mb, wv_ref[...]).astype(BF16)


def _kv_proj(mem2d, g, wk, wv, tm=256):
    R = mem2d.shape[0]
    row = lambda i: (i, 0)
    fixed = lambda i: (0, 0)
    return pl.pallas_call(
        _kv_proj_kernel,
        grid=(R // tm,),
        in_specs=[pl.BlockSpec((tm, D_MODEL), row), pl.BlockSpec((1, D_MODEL), fixed),
                  pl.BlockSpec((D_MODEL, D_MODEL), fixed), pl.BlockSpec((D_MODEL, D_MODEL), fixed)],
        out_specs=[pl.BlockSpec((tm, D_MODEL), row)] * 2,
        out_shape=[jax.ShapeDtypeStruct((R, D_MODEL), BF16)] * 2,
        compiler_params=_cparams("parallel"),
        name="kv_proj",
    )(mem2d, g, wk, wv)


def _out_proj_kernel(y5_ref, yg_ref, yr_ref, yc_ref, beta_ref, w_ref, h_ref, o_ref):
    acc = h_ref[...]
    for i, y_ref in enumerate((y5_ref, yg_ref, yr_ref, yc_ref)):
        lo = i * W_GROUP
        yb = (y_ref[...] * beta_ref[:, lo:lo + W_GROUP]).astype(BF16)
        acc = acc + _dg(yb, w_ref[lo:lo + W_GROUP, :])
    o_ref[...] = acc


def _out_proj(ys, beta, w, h, tm=512):
    T = h.shape[0]
    row = lambda i: (i, 0)
    fixed = lambda i: (0, 0)
    return pl.pallas_call(
        _out_proj_kernel,
        grid=(T // tm,),
        in_specs=[pl.BlockSpec((tm, W_GROUP), row)] * 4
        + [pl.BlockSpec((1, D_MODEL), fixed), pl.BlockSpec((D_MODEL, D_MODEL), fixed),
           pl.BlockSpec((tm, D_MODEL), row)],
        out_specs=pl.BlockSpec((tm, D_MODEL), row),
        out_shape=jax.ShapeDtypeStruct((T, D_MODEL), F32),
        compiler_params=_cparams("parallel"),
        name="out_proj",
    )(*ys, beta, w, h)


def _final_norm_kernel(x_ref, g_ref, o_ref):
    o_ref[...] = _rms(x_ref[...], g_ref[...])


def _final_norm(h, g, tm=1024):
    T = h.shape[0]
    return pl.pallas_call(
        _final_norm_kernel,
        grid=(T // tm,),
        in_specs=[pl.BlockSpec((tm, D_MODEL), lambda i: (i, 0)), pl.BlockSpec((1, D_MODEL), lambda i: (0, 0))],
        out_specs=pl.BlockSpec((tm, D_MODEL), lambda i: (i, 0)),
        out_shape=jax.ShapeDtypeStruct((T, D_MODEL), F32),
        compiler_params=_cparams("parallel"),
        name="final_norm",
    )(h, g)


def _s5_kernel(u_ref, bbd_ref, cbd_ref, lam_ref, pw_ref, d_ref, gw_ref, gb_ref, o_ref, st_ref, carry_ref):
    n = S5_NS
    tl = u_ref.shape[0]

    @pl.when(pl.program_id(1) == 0)
    def _():
        carry_ref[...] = jnp.zeros_like(carry_ref)

    u = u_ref[...]
    st_ref[...] = _dg(u.astype(BF16), bbd_ref[...])

    steps = [(1, lam_ref[0:1, :], lam_ref[1:2, :]), (2, lam_ref[2:3, :], lam_ref[3:4, :]),
             (4, lam_ref[4:5, :], lam_ref[5:6, :])]
    pw_re = pw_ref[:, 0:n]
    pw_im = pw_ref[:, n:2 * n]

    def body(i, carry):
        c_re, c_im = carry
        r0 = pl.multiple_of(i * 8, 8)
        x_re = st_ref[pl.ds(r0, 8), 0:n]
        x_im = st_ref[pl.ds(r0, 8), n:2 * n]
        for s, l_re, l_im in steps:
            s_re = _shift_rows(x_re, s)
            s_im = _shift_rows(x_im, s)
            x_re, x_im = x_re + (l_re * s_re - l_im * s_im), x_im + (l_re * s_im + l_im * s_re)
        x_re, x_im = x_re + (pw_re * c_re - pw_im * c_im), x_im + (pw_re * c_im + pw_im * c_re)
        st_ref[pl.ds(r0, 8), 0:n] = x_re
        st_ref[pl.ds(r0, 8), n:2 * n] = x_im
        return (jnp.broadcast_to(x_re[7:8, :], (8, n)), jnp.broadcast_to(x_im[7:8, :], (8, n)))

    c_re, c_im = lax.fori_loop(0, tl // 8, body, (carry_ref[:, 0:n], carry_ref[:, n:2 * n]))
    carry_ref[:, 0:n] = c_re
    carry_ref[:, n:2 * n] = c_im

    y = _dg(st_ref[...].astype(BF16), cbd_ref[...]) + d_ref[...] * u
    y = 0.5 * y * (1.0 + jnp.tanh(math.sqrt(2.0 / math.pi) * (y + 0.044715 * (y * y * y))))
    gate = _dg(y.astype(BF16), gw_ref[...]) + gb_ref[...]
    o_ref[...] = y * _sigmoid(gate)


def _s5_mixer(p, prm, B, L, tl=256):
    bbd, cbd, lam_tab, pw_tab, d, gw, gb = prm
    fixed = lambda b, l: (0, 0)
    return pl.pallas_call(
        _s5_kernel,
        grid=(B, L // tl),
        in_specs=[pl.BlockSpec((None, tl, W_GROUP), lambda b, l: (b, l, 0)),
                  pl.BlockSpec(bbd.shape, fixed), pl.BlockSpec(cbd.shape, fixed),
                  pl.BlockSpec(lam_tab.shape, fixed), pl.BlockSpec(pw_tab.shape, fixed),
                  pl.BlockSpec(d.shape, fixed), pl.BlockSpec(gw.shape, fixed), pl.BlockSpec(gb.shape, fixed)],
        out_specs=pl.BlockSpec((None, tl, W_GROUP), lambda b, l: (b, l, 0)),
        out_shape=jax.ShapeDtypeStruct((B, L, W_GROUP), F32),
        scratch_shapes=[pltpu.VMEM((tl, 2 * S5_NS), F32), pltpu.VMEM((8, 2 * S5_NS), F32)],
        compiler_params=_cparams("parallel", "arbitrary"),
        name="s5_mixer",
    )(p.reshape(B, L, W_GROUP), bbd, cbd, lam_tab, pw_tab, d, gw, gb)


def _s5_params(lam_re, lam_im, b_re, b_im, c_re, c_im, d_skip, log_dt, glu_w, glu_b):
    G, N, P = S5_GROUPS, S5_STATE, S5_GROUP_CH
    dt = jnp.exp(log_dt)[:, None]
    mag = jnp.exp(lam_re * dt)
    lb_re = mag * jnp.cos(lam_im * dt)
    lb_im = mag * jnp.sin(lam_im * dt)
    den = lam_re * lam_re + lam_im * lam_im
    n_re, n_im = lb_re - 1.0, lb_im
    f_re = (n_re * lam_re + n_im * lam_im) / den
    f_im = (n_im * lam_re - n_re * lam_im) / den
    bb_re = f_re[:, :, None] * b_re - f_im[:, :, None] * b_im
    bb_im = f_re[:, :, None] * b_im + f_im[:, :, None] * b_re
    eye = jnp.eye(G, dtype=F32)
    bbd_re = jnp.einsum('gnp,gh->gphn', bb_re, eye).reshape(G * P, G * N)
    bbd_im = jnp.einsum('gnp,gh->gphn', bb_im, eye).reshape(G * P, G * N)
    bbd = jnp.concatenate([bbd_re, bbd_im], axis=1).astype(BF16)
    cbd_re = jnp.einsum('gpn,gh->gnhp', c_re, eye).reshape(G * N, G * P)
    cbd_im = jnp.einsum('gpn,gh->gnhp', -c_im, eye).reshape(G * N, G * P)
    cbd = jnp.concatenate([cbd_re, cbd_im], axis=0).astype(BF16)

    def cmul(a, b):
        return a[0] * b[0] - a[1] * b[1], a[0] * b[1] + a[1] * b[0]

    l1 = (lb_re.reshape(1, G * N), lb_im.reshape(1, G * N))
    l2 = cmul(l1, l1)
    l4 = cmul(l2, l2)
    zero = jnp.zeros_like(l1[0])
    lam_tab = jnp.concatenate([l1[0], l1[1], l2[0], l2[1], l4[0], l4[1], zero, zero], axis=0)
    pws = [l1]
    for _ in range(7):
        pws.append(cmul(pws[-1], l1))
    pw_tab = jnp.concatenate([jnp.concatenate([p[0] for p in pws], axis=0),
                              jnp.concatenate([p[1] for p in pws], axis=0)], axis=1)
    return (bbd, cbd, lam_tab, pw_tab, d_skip.reshape(1, W_GROUP), glu_w.astype(BF16),
            glu_b.reshape(1, W_GROUP))


def _gla_kernel(p_ref, wup_ref, bup_ref, tri_ref, sege_ref, segv_ref, ng_ref, o_ref,
                st_ref, b_ref, oacc_ref):
    C = GLA_CHUNK
    tl = p_ref.shape[0]

    @pl.when(pl.program_id(1) == 0)
    def _():
        st_ref[...] = jnp.zeros_like(st_ref)

    z = p_ref[:, 768:896]
    x = _dg(z.astype(BF16), wup_ref[...]) + bup_ref[...]
    log_alpha = (jnp.minimum(x, 0.0) - jnp.log1p(jnp.exp(-jnp.abs(x)))) * (1.0 / GLA_TAU)
    b_ref[...] = _dot_exact_lhs(tri_ref[...], log_alpha)

    row_i = _iota((C, 128), 0)
    bd_mask = (_iota((256, 128), 0) >> 6) == (_iota((256, 128), 1) >> 5)

    def chunk(c, _):
        r0 = pl.multiple_of(c * C, C)
        q = p_ref[pl.ds(r0, C), 0:128] * (GLA_DK ** -0.5)
        k = p_ref[pl.ds(r0, C), 128:256]
        v = p_ref[pl.ds(r0, C), 256:512]
        b = b_ref[pl.ds(r0, C), :]
        parts = []
        for j in range(C):
            e = jnp.exp(b - b[j:j + 1, :])
            parts.append(jnp.where(row_i >= j, (q * k[j:j + 1, :]) * e, 0.0).astype(BF16))
        t = jnp.concatenate(parts, axis=0)
        w = _dg(t, sege_ref[...])
        o = w[0:C, :] * v[0:1, :]
        for j in range(1, C):
            o = o + w[j * C:(j + 1) * C, :] * v[j:j + 1, :]
        st = st_ref[...]
        qe = (q * jnp.exp(b)).astype(BF16)
        o = o + _dg(qe, st.astype(BF16), _NT)
        b_last = b[C - 1:C, :]
        kt = (k * jnp.exp(b_last - b)).astype(BF16)
        upd = _dg(v.astype(BF16), kt, _TN)
        st_ref[...] = st * jnp.exp(b_last) + jnp.where(bd_mask, upd, 0.0)
        oacc_ref[pl.ds(r0, C), :] = o
        return 0

    lax.fori_loop(0, tl // C, chunk, 0)

    o = oacc_ref[...]
    ms = _dot_exact_rhs(o * o, segv_ref[...]) * (1.0 / GLA_DV)
    g = p_ref[:, 512:768]
    o_ref[...] = o * lax.rsqrt(ms + EPS) * ng_ref[...] * (g * _sigmoid(g))


def _gla_mixer(p, prm, B, L, tl=256):
    wup, bup, ng = prm
    C = GLA_CHUNK
    r = jnp.arange(tl)
    tri = ((r[:, None] // C == r[None, :] // C) & (r[None, :] <= r[:, None])).astype(BF16)
    sege = (jnp.arange(128)[:, None] // GLA_DK == jnp.arange(256)[None, :] // GLA_DV).astype(BF16)
    segv = (jnp.arange(256)[:, None] // GLA_DV == jnp.arange(256)[None, :] // GLA_DV).astype(BF16)
    fixed = lambda b, l: (0, 0)
    return pl.pallas_call(
        _gla_kernel,
        grid=(B, L // tl),
        in_specs=[pl.BlockSpec((None, tl, GLA_COLS_PAD), lambda b, l: (b, l, 0)),
                  pl.BlockSpec(wup.shape, fixed), pl.BlockSpec(bup.shape, fixed),
                  pl.BlockSpec(tri.shape, fixed), pl.BlockSpec(sege.shape, fixed),
                  pl.BlockSpec(segv.shape, fixed), pl.BlockSpec(ng.shape, fixed)],
        out_specs=pl.BlockSpec((None, tl, W_GROUP), lambda b, l: (b, l, 0)),
        out_shape=jax.ShapeDtypeStruct((B, L, W_GROUP), F32),
        scratch_shapes=[pltpu.VMEM((256, 128), F32), pltpu.VMEM((tl, 128), F32), pltpu.VMEM((tl, 256), F32)],
        compiler_params=_cparams("parallel", "arbitrary"),
        name="gla_mixer",
    )(p.reshape(B, L, GLA_COLS_PAD), wup, bup, tri, sege, segv, ng)


def _gla_params(w_up, b_up, norm_g):
    wup = jnp.zeros((128, 128), F32).at[:GLA_RANK, :].set(w_up).astype(BF16)
    return wup, b_up.reshape(1, 128), norm_g.reshape(1, W_GROUP)


def _rw_kernel(p_ref, mu_ref, w0_ref, w2_ref, a0_ref, a2_ref, g2_ref, kk_ref, ka_ref, rk_ref,
               lng_ref, lnb_ref, seg_ref, tri_ref, o_ref, st_ref, prev_ref):
    C = RW_CHUNK
    W = W_GROUP

    @pl.when(pl.program_id(1) == 0)
    def _():
        st_ref[...] = jnp.zeros_like(st_ref)
        prev_ref[...] = jnp.zeros_like(prev_ref)

    p = p_ref[...]
    prev_row = jnp.broadcast_to(prev_ref[0:1, :], p.shape)
    p_prev = _shift_rows(p, 1, prev_row)
    prev_ref[0:1, :] = p[C - 1:C, :]
    xm = p + (p_prev - p) * mu_ref[...]
    r = xm[:, 0:W]
    k = xm[:, W:2 * W]
    v = xm[:, 2 * W:3 * W]
    z = xm[:, 3 * W:3 * W + 128]

    w = -_softplus(-(w0_ref[...] + _dg(jnp.tanh(z).astype(BF16), w2_ref[...]))) - 0.5
    logd = -jnp.exp(w)
    alr = _sigmoid(a0_ref[...] + _dg(z.astype(BF16), a2_ref[...]))
    g = _dg(_sigmoid(z).astype(BF16), g2_ref[...])

    seg = seg_ref[...]
    kk = k * kk_ref[...]
    kk = kk / jnp.maximum(jnp.sqrt(_dot_exact_rhs(kk * kk, seg)), 1e-12)
    k2 = k * (1.0 + (alr - 1.0) * ka_ref[...])
    av = -kk
    bv = kk * alr
    bonus = _dot_exact_rhs(r * k2 * rk_ref[...], seg) * v

    lp = _dot_exact_lhs(tri_ref[...], logd)
    lp_c = lp[C - 1:C, :]
    e_neg = jnp.exp(-lp)
    e_rem = jnp.exp(lp_c - lp)
    ah = av * jnp.exp(lp - logd)
    rh = r * jnp.exp(lp)
    bh = bv * e_neg
    kh = k2 * e_neg
    b2 = bv * e_rem
    kc = k2 * e_rem
    p_c = jnp.exp(lp_c)

    lane_head = _iota((C, W), 1) >> 6
    hm = [lane_head == h for h in range(RW_HEADS)]
    zeros = jnp.zeros((C, W), F32)
    lhs = jnp.concatenate([jnp.where(m, ah, zeros) for m in hm] + [jnp.where(m, rh, zeros) for m in hm], axis=0)
    xb = _dot3(lhs, bh, _NT).reshape(8, C, C)
    xk = _dot3(lhs, kh, _NT).reshape(8, C, C)

    ti = _iota((RW_HEADS, C, C), 1)
    tj = _iota((RW_HEADS, C, C), 2)
    strict = ti > tj
    incl = ti >= tj
    lm = jnp.where(strict, xb[0:4], 0.0)
    mak = jnp.where(strict, xk[0:4], 0.0)
    nrb = jnp.where(incl, xb[4:8], 0.0)
    nrk = jnp.where(incl, xk[4:8], 0.0)

    def bmm(a, b):
        return _dot3(a, b, _BNN)

    tinv = jnp.where(ti == tj, 1.0, 0.0) + lm
    lpow = lm
    for _ in range(5):
        lpow = bmm(lpow, lpow)
        tinv = tinv + bmm(tinv, lpow)

    def apply(m, x):
        full = _dot3(m.reshape(RW_HEADS * C, C), x).reshape(RW_HEADS, C, W)
        out = jnp.where(hm[0], full[0], zeros)
        for h in range(1, RW_HEADS):
            out = out + jnp.where(hm[h], full[h], zeros)
        return out

    wv = apply(bmm(tinv, mak), v)
    a2 = apply(tinv, ah)
    r2 = rh + apply(nrb, a2)
    y0 = apply(nrb, wv) + apply(nrk, v)

    bd = (_iota((W, W), 0) >> 6) == (_iota((W, W), 1) >> 6)
    eye = _iota((W, W), 0) == _iota((W, W), 1)
    a_c = jnp.where(eye, jnp.broadcast_to(p_c, (W, W)), 0.0) + jnp.where(bd, _dot3(b2, a2, _TN), 0.0)
    g0 = jnp.where(bd, _dot3(b2, wv, _TN) + _dot3(kc, v, _TN), 0.0)

    s0 = st_ref[...]
    y = _dot3(r2, s0) + y0
    st_ref[...] = _dot3(a_c, s0) + g0

    mean = _dot_exact_rhs(y, seg) * (1.0 / RW_N)
    yc = y - mean
    var = _dot_exact_rhs(yc * yc, seg) * (1.0 / RW_N)
    yn = yc * lax.rsqrt(var + RW_LN_EPS) * lng_ref[...] + lnb_ref[...]
    o_ref[...] = (yn + bonus) * g


def _rw_mixer(p, prm, B, L):
    C = RW_CHUNK
    seg = (jnp.arange(256)[:, None] // RW_N == jnp.arange(256)[None, :] // RW_N).astype(BF16)
    tri = (jnp.arange(C)[None, :] <= jnp.arange(C)[:, None]).astype(BF16)
    fixed = lambda b, l: (0, 0)
    consts = tuple(prm) + (seg, tri)
    return pl.pallas_call(
        _rw_kernel,
        grid=(B, L // C),
        in_specs=[pl.BlockSpec((None, C, RW_COLS), lambda b, l: (b, l, 0))]
        + [pl.BlockSpec(c.shape, fixed) for c in consts],
        out_specs=pl.BlockSpec((None, C, W_GROUP), lambda b, l: (b, l, 0)),
        out_shape=jax.ShapeDtypeStruct((B, L, W_GROUP), F32),
        scratch_shapes=[pltpu.VMEM((W_GROUP, W_GROUP), F32), pltpu.VMEM((8, RW_COLS), F32)],
        compiler_params=_cparams("parallel", "arbitrary"),
        name="rwkv7_mixer",
    )(p.reshape(B, L, RW_COLS), *consts)


def _rw_params(mu, w0, w2, a0, a2, g2, k_k, k_a, r_k, ln_g, ln_b):
    row = lambda t: t.reshape(1, -1)
    w2p = jnp.zeros((128, W_GROUP), F32).at[0:RW_W_RANK].set(w2).astype(BF16)
    a2p = jnp.zeros((128, W_GROUP), F32).at[RW_W_RANK:RW_W_RANK + RW_A_RANK].set(a2).astype(BF16)
    g2p = jnp.zeros((128, W_GROUP), F32).at[RW_W_RANK + RW_A_RANK:].set(g2).astype(BF16)
    return (row(mu), row(w0), w2p, row(a0), a2p, g2p, row(k_k), row(k_a), row(r_k), row(ln_g), row(ln_b))


def _conv_kernel(p_ref, w_ref, b_ref, lg_ref, lb_ref, o_ref, u_ref):
    tl = p_ref.shape[0]
    H = CONV_HALO
    RC = 64

    @pl.when(pl.program_id(1) == 0)
    def _():
        u_ref[0:H, :] = jnp.zeros((H, W_GROUP), F32)

    u_ref[H:H + tl, :] = p_ref[:, 0:W_GROUP] * _sigmoid(p_ref[:, W_GROUP:2 * W_GROUP])
    off = H - (CONV_WIDTH - 1)
    for c in range(tl // RC):
        acc = jnp.zeros((RC, W_GROUP), F32)
        for j in range(CONV_WIDTH):
            acc = acc + w_ref[j:j + 1, :] * u_ref[c * RC + off + j:c * RC + off + j + RC, :]
        y = acc + b_ref[...]
        mu = jnp.mean(y, axis=-1, keepdims=True)
        yc = y - mu
        yn = yc * lax.rsqrt(jnp.mean(yc * yc, axis=-1, keepdims=True) + CONV_LN_EPS) * lg_ref[...] + lb_ref[...]
        o_ref[c * RC:(c + 1) * RC, :] = yn * _sigmoid(yn)
    tail = u_ref[tl:tl + H, :]
    u_ref[0:H, :] = tail


def _conv_mixer(p, prm, B, L, tl=512):
    w, b, lg, lb = prm
    fixed = lambda b_, l: (0, 0)
    return pl.pallas_call(
        _conv_kernel,
        grid=(B, L // tl),
        in_specs=[pl.BlockSpec((None, tl, 2 * W_GROUP), lambda b_, l: (b_, l, 0)),
                  pl.BlockSpec(w.shape, fixed), pl.BlockSpec(b.shape, fixed),
                  pl.BlockSpec(lg.shape, fixed), pl.BlockSpec(lb.shape, fixed)],
        out_specs=pl.BlockSpec((None, tl, W_GROUP), lambda b_, l: (b_, l, 0)),
        out_shape=jax.ShapeDtypeStruct((B, L, W_GROUP), F32),
        scratch_shapes=[pltpu.VMEM((tl + CONV_HALO, W_GROUP), F32)],
        compiler_params=_cparams("parallel", "arbitrary"),
        name="conv_mixer",
    )(p.reshape(B, L, 2 * W_GROUP), w, b, lg, lb)


def _xattn_kernel(h_ref, g_ref, wq_ref, k_ref, v_ref, wo_ref, o_ref):
    x = h_ref[...]
    q = _dg(_rms(x, g_ref[...]).astype(BF16), wq_ref[...])
    acc = x
    for hd in range(X_HEADS):
        lo = hd * X_HEAD_DIM
        s = _dg(q[:, lo:lo + X_HEAD_DIM].astype(BF16), k_ref[:, lo:lo + X_HEAD_DIM], _NT) * (X_HEAD_DIM ** -0.5)
        e = jnp.exp(s - jnp.max(s, axis=-1, keepdims=True))
        pr = e / jnp.sum(e, axis=-1, keepdims=True)
        oh = _dg(pr.astype(BF16), v_ref[:, lo:lo + X_HEAD_DIM])
        acc = acc + _dg(oh.astype(BF16), wo_ref[lo:lo + X_HEAD_DIM, :])
    o_ref[...] = acc


def _xattn(h, g, wq, k, v, wo, B, L, n_mem, tq=512):
    fixed = lambda b, l: (0, 0)
    out = pl.pallas_call(
        _xattn_kernel,
        grid=(B, L // tq),
        in_specs=[pl.BlockSpec((None, tq, D_MODEL), lambda b, l: (b, l, 0)),
                  pl.BlockSpec((1, D_MODEL), fixed), pl.BlockSpec((D_MODEL, D_MODEL), fixed),
                  pl.BlockSpec((None, n_mem, D_MODEL), lambda b, l: (b, 0, 0)),
                  pl.BlockSpec((None, n_mem, D_MODEL), lambda b, l: (b, 0, 0)),
                  pl.BlockSpec((D_MODEL, D_MODEL), fixed)],
        out_specs=pl.BlockSpec((None, tq, D_MODEL), lambda b, l: (b, l, 0)),
        out_shape=jax.ShapeDtypeStruct((B, L, D_MODEL), F32),
        compiler_params=_cparams("parallel", "parallel"),
        name="mem_xattn",
    )(h.reshape(B, L, D_MODEL), g, wq, k.reshape(B, n_mem, D_MODEL), v.reshape(B, n_mem, D_MODEL), wo)
    return out.reshape(B * L, D_MODEL)


def _router_kernel(h_ref, g_ref, wr_ref, br_ref, xn_ref, eid_ref, gate_ref):
    xn = _rms(h_ref[...], g_ref[...])
    xn_ref[...] = xn
    lg = _dot3(wr_ref[...], xn, _NT) + br_ref[...]
    tm = lg.shape[1]
    gl = lg[0:8, :]
    sub = _iota((8, tm), 0)
    gmax = jnp.max(gl, axis=0, keepdims=True)
    gsel = jnp.min(jnp.where(gl == gmax, sub, 8), axis=0, keepdims=True)
    g_w = 1.0 / jnp.sum(jnp.exp(gl - gmax), axis=0, keepdims=True)
    e_in = jnp.zeros((8, tm), F32)
    for grp in range(N_GROUPS):
        e_in = e_in + jnp.where(gsel == grp, lg[8 + 8 * grp:16 + 8 * grp, :], 0.0)
    t1 = jnp.max(e_in, axis=0, keepdims=True)
    i1 = jnp.min(jnp.where(e_in == t1, sub, 8), axis=0, keepdims=True)
    rest = jnp.where(sub == i1, -jnp.inf, e_in)
    t2 = jnp.max(rest, axis=0, keepdims=True)
    i2 = jnp.min(jnp.where(rest == t2, sub, 8), axis=0, keepdims=True)
    e2 = jnp.exp(t2 - t1)
    p1 = 1.0 / (1.0 + e2)
    eid_ref[0:1, :] = gsel * EXP_PER_GROUP + i1
    eid_ref[1:2, :] = gsel * EXP_PER_GROUP + i2
    gate_ref[0:1, :] = p1 * g_w
    gate_ref[1:2, :] = (e2 * p1) * g_w


def _router(h, g, wr, br, tm=ROUTE_TILE):
    T = h.shape[0]
    nt = T // tm
    return pl.pallas_call(
        _router_kernel,
        grid=(nt,),
        in_specs=[pl.BlockSpec((tm, D_MODEL), lambda i: (i, 0)), pl.BlockSpec((1, D_MODEL), lambda i: (0, 0)),
                  pl.BlockSpec(wr.shape, lambda i: (0, 0)), pl.BlockSpec(br.shape, lambda i: (0, 0))],
        out_specs=[pl.BlockSpec((tm, D_MODEL), lambda i: (i, 0)),
                   pl.BlockSpec((None, 2, tm), lambda i: (i, 0, 0)),
                   pl.BlockSpec((None, 2, tm), lambda i: (i, 0, 0))],
        out_shape=[jax.ShapeDtypeStruct((T, D_MODEL), F32),
                   jax.ShapeDtypeStruct((nt, 2, tm), jnp.int32),
                   jax.ShapeDtypeStruct((nt, 2, tm), F32)],
        compiler_params=_cparams("parallel"),
        name="moe_router",
    )(h, g, wr, br)


def _rank_kernel(eid_ref, ut_ref, rank_ref, cnt_ref, carry_ref):
    tm = eid_ref.shape[1]

    @pl.when(pl.program_id(0) == 0)
    def _():
        carry_ref[...] = jnp.zeros_like(carry_ref)

    ex = _iota((N_EXPERTS, tm), 0)
    oh0 = jnp.where(ex == eid_ref[0:1, :], 1.0, 0.0)
    oh1 = jnp.where(ex == eid_ref[1:2, :], 1.0, 0.0)
    oh = oh0 + oh1
    cum = _dg(oh.astype(BF16), ut_ref[...])
    carry = carry_ref[:, 0:1]
    before = cum - oh + carry
    rank_ref[0:1, :] = jnp.sum(oh0 * before, axis=0, keepdims=True).astype(jnp.int32)
    rank_ref[1:2, :] = jnp.sum(oh1 * before, axis=0, keepdims=True).astype(jnp.int32)
    total = carry + cum[:, tm - 1:tm]
    carry_ref[...] = jnp.broadcast_to(total, carry_ref.shape)
    cnt_ref[...] = jnp.broadcast_to(total, cnt_ref.shape).astype(jnp.int32)


def _rank(eid, tm=ROUTE_TILE):
    nt = eid.shape[0]
    ut = (jnp.arange(tm)[:, None] <= jnp.arange(tm)[None, :]).astype(BF16)
    return pl.pallas_call(
        _rank_kernel,
        grid=(nt,),
        in_specs=[pl.BlockSpec((None, 2, tm), lambda i: (i, 0, 0)), pl.BlockSpec((tm, tm), lambda i: (0, 0))],
        out_specs=[pl.BlockSpec((None, 2, tm), lambda i: (i, 0, 0)),
                   pl.BlockSpec((N_EXPERTS, 128), lambda i: (0, 0))],
        out_shape=[jax.ShapeDtypeStruct((nt, 2, tm), jnp.int32),
                   jax.ShapeDtypeStruct((N_EXPERTS, 128), jnp.int32)],
        scratch_shapes=[pltpu.VMEM((N_EXPERTS, 128), F32)],
        compiler_params=_cparams("arbitrary"),
        name="moe_rank",
    )(eid, ut)


def _dispatch_kernel(ps_ref, eid_ref, rank_ref, x_ref, xs_in_ref, xs_ref, sem):
    del xs_in_ref
    tm = x_ref.shape[0]

    def copy(r, k):
        dest = ps_ref[eid_ref[k, r]] + rank_ref[k, r]
        return pltpu.make_async_copy(x_ref.at[pl.ds(r, 1)], xs_ref.at[pl.ds(dest, 1)], sem)

    def issue(r, _):
        copy(r, 0).start()
        copy(r, 1).start()
        return 0

    def drain(r, _):
        copy(r, 0).wait()
        copy(r, 1).wait()
        return 0

    lax.fori_loop(0, tm, issue, 0)
    lax.fori_loop(0, tm, drain, 0)


def _dispatch(pad_starts, eid, rank, xn, xs_zero, tm=ROUTE_TILE):
    T = xn.shape[0]
    smem_blk = pl.BlockSpec((None, 2, tm), lambda i, ps: (i, 0, 0), memory_space=pltpu.SMEM)
    return pl.pallas_call(
        _dispatch_kernel,
        grid_spec=pltpu.PrefetchScalarGridSpec(
            num_scalar_prefetch=1,
            grid=(T // tm,),
            in_specs=[smem_blk, smem_blk, pl.BlockSpec((tm, D_MODEL), lambda i, ps: (i, 0)),
                      pl.BlockSpec(memory_space=pl.ANY)],
            out_specs=pl.BlockSpec(memory_space=pl.ANY),
            scratch_shapes=[pltpu.SemaphoreType.DMA(())],
        ),
        out_shape=jax.ShapeDtypeStruct(xs_zero.shape, F32),
        input_output_aliases={4: 0},
        compiler_params=_cparams("arbitrary"),
        name="moe_dispatch",
    )(pad_starts, eid, rank, xn, xs_zero)


def _expert_kernel(be_ref, x_ref, wg_ref, wu_ref, wd_ref, y_ref):
    del be_ref
    xb = x_ref[...].astype(BF16)
    gate = _dg(xb, wg_ref[...])
    hid = (gate * _sigmoid(gate)) * _dg(xb, wu_ref[...])
    y_ref[...] = _dg(hid.astype(BF16), wd_ref[...])


def _experts(block_exp, xs, w_gate, w_up, w_down):
    n_slots = xs.shape[0]
    nb = n_slots // MOE_BLOCK
    return pl.pallas_call(
        _expert_kernel,
        grid_spec=pltpu.PrefetchScalarGridSpec(
            num_scalar_prefetch=1,
            grid=(nb,),
            in_specs=[pl.BlockSpec((MOE_BLOCK, D_MODEL), lambda i, be: (i, 0)),
                      pl.BlockSpec((None, D_MODEL, D_EXPERT), lambda i, be: (be[i], 0, 0)),
                      pl.BlockSpec((None, D_MODEL, D_EXPERT), lambda i, be: (be[i], 0, 0)),
                      pl.BlockSpec((None, D_EXPERT, D_MODEL), lambda i, be: (be[i], 0, 0))],
            out_specs=pl.BlockSpec((MOE_BLOCK, D_MODEL), lambda i, be: (i, 0)),
        ),
        out_shape=jax.ShapeDtypeStruct((n_slots, D_MODEL), F32),
        compiler_params=_cparams("arbitrary"),
        name="moe_experts",
    )(block_exp, xs, w_gate, w_up, w_down)


def _combine_kernel(ps_ref, eid_ref, rank_ref, gate_ref, h_ref, ys_ref, o_ref, y0_ref, y1_ref, sem):
    tm = h_ref.shape[0]
    bufs = (y0_ref, y1_ref)

    def copy(r, k):
        src = ps_ref[eid_ref[k, r]] + rank_ref[k, r]
        return pltpu.make_async_copy(ys_ref.at[pl.ds(src, 1)], bufs[k].at[pl.ds(r, 1)], sem)

    def issue(r, _):
        copy(r, 0).start()
        copy(r, 1).start()
        return 0

    def drain(r, _):
        copy(r, 0).wait()
        copy(r, 1).wait()
        return 0

    lax.fori_loop(0, tm, issue, 0)
    lax.fori_loop(0, tm, drain, 0)
    o_ref[...] = h_ref[...] + gate_ref[:, 0:1] * y0_ref[...] + gate_ref[:, 1:2] * y1_ref[...]


def _combine(pad_starts, eid, rank, gate_t, h, ys, tm=ROUTE_TILE):
    T = h.shape[0]
    smem_blk = pl.BlockSpec((None, 2, tm), lambda i, ps: (i, 0, 0), memory_space=pltpu.SMEM)
    return pl.pallas_call(
        _combine_kernel,
        grid_spec=pltpu.PrefetchScalarGridSpec(
            num_scalar_prefetch=1,
            grid=(T // tm,),
            in_specs=[smem_blk, smem_blk, pl.BlockSpec((tm, 2), lambda i, ps: (i, 0)),
                      pl.BlockSpec((tm, D_MODEL), lambda i, ps: (i, 0)),
                      pl.BlockSpec(memory_space=pl.ANY)],
            out_specs=pl.BlockSpec((tm, D_MODEL), lambda i, ps: (i, 0)),
            scratch_shapes=[pltpu.VMEM((tm, D_MODEL), F32), pltpu.VMEM((tm, D_MODEL), F32),
                            pltpu.SemaphoreType.DMA(())],
        ),
        out_shape=jax.ShapeDtypeStruct((T, D_MODEL), F32),
        compiler_params=_cparams("arbitrary"),
        name="moe_combine",
    )(pad_starts, eid, rank, gate_t, h, ys)


def _moe(h, g, group_w, group_b, expert_w, expert_b, w_gate, w_up, w_down):
    T = h.shape[0]
    wr = jnp.zeros((ROUTE_ROWS, D_MODEL), F32).at[0:N_GROUPS].set(group_w.T).at[8:].set(expert_w.T)
    br = jnp.full((ROUTE_ROWS, 1), -1e30, F32).at[0:N_GROUPS, 0].set(group_b).at[8:, 0].set(expert_b)
    xn, eid, gate = _router(h, g, wr, br)
    rank, cnt = _rank(eid)
    counts = cnt[:, 0]
    padded = (counts + MOE_BLOCK - 1) // MOE_BLOCK * MOE_BLOCK
    pad_ends = jnp.cumsum(padded)
    pad_starts = (pad_ends - padded).astype(jnp.int32)
    n_slots = ((T * 2 + MOE_BLOCK - 1) // MOE_BLOCK + N_EXPERTS) * MOE_BLOCK
    block_start = jnp.arange(n_slots // MOE_BLOCK, dtype=jnp.int32) * MOE_BLOCK
    block_exp = jnp.minimum(jnp.searchsorted(pad_ends, block_start, side='right'), N_EXPERTS - 1).astype(jnp.int32)
    xs = _dispatch(pad_starts, eid, rank, xn, jnp.zeros((n_slots, D_MODEL), F32))
    ys = _experts(block_exp, xs, w_gate, w_up, w_down)
    gate_t = gate.transpose(0, 2, 1).reshape(T, 2)
    return _combine(pad_starts, eid, rank, gate_t, h, ys)


def kernel(x, mem, norm_mix_g, w_in, w_out, mix_beta, s5_lam_re, s5_lam_im, s5_b_re, s5_b_im, s5_c_re, s5_c_im, s5_d, s5_log_dt, s5_glu_w, s5_glu_b, gla_w_up, gla_b_up, gla_norm_g, rw_mu, rw_w0, rw_w2, rw_a0, rw_a2, rw_g2, rw_k_k, rw_k_a, rw_r_k, rw_ln_g, rw_ln_b, conv_w, conv_b, conv_ln_g, conv_ln_b, norm_xattn_g, norm_mem_g, xa_wq, xa_wk, xa_wv, xa_wo, norm_ffn_g, moe_group_w, moe_group_b, moe_expert_w, moe_expert_b, moe_w_gate, moe_w_up, moe_w_down, norm_final_g):
    B, L, D = x.shape
    n_mem = mem.shape[1]
    depth = w_in.shape[0]
    T = B * L
    h = x.reshape(T, D)
    mem2d = mem.reshape(B * n_mem, D)
    row = lambda t: t.reshape(1, -1)
    c0 = W_GROUP
    c1 = c0 + 784
    c2 = c1 + RW_COLS
    for l in range(depth):
        wl = w_in[l]
        w5 = wl[:, :c0].astype(BF16)
        wg = jnp.pad(wl[:, c0:c1], ((0, 0), (0, GLA_COLS_PAD - 784))).astype(BF16)
        wr = wl[:, c1:c2].astype(BF16)
        wc = wl[:, c2:].astype(BF16)
        p5, pg, pr, pc = _in_proj(h, row(norm_mix_g[l]), w5, wg, wr, wc)
        y5 = _s5_mixer(p5, _s5_params(s5_lam_re[l], s5_lam_im[l], s5_b_re[l], s5_b_im[l], s5_c_re[l], s5_c_im[l],
                                      s5_d[l], s5_log_dt[l], s5_glu_w[l], s5_glu_b[l]), B, L)
        yg = _gla_mixer(pg, _gla_params(gla_w_up[l], gla_b_up[l], gla_norm_g[l]), B, L)
        yr = _rw_mixer(pr, _rw_params(rw_mu[l], rw_w0[l], rw_w2[l], rw_a0[l], rw_a2[l], rw_g2[l], rw_k_k[l],
                                      rw_k_a[l], rw_r_k[l], rw_ln_g[l], rw_ln_b[l]), B, L)
        yc = _conv_mixer(pc, (conv_w[l], row(conv_b[l]), row(conv_ln_g[l]), row(conv_ln_b[l])), B, L)
        ys = [t.reshape(T, W_GROUP) for t in (y5, yg, yr, yc)]
        h = _out_proj(ys, row(mix_beta[l]), w_out[l].astype(BF16), h)
        kmem, vmem = _kv_proj(mem2d, row(norm_mem_g[l]), xa_wk[l].astype(BF16), xa_wv[l].astype(BF16))
        h = _xattn(h, row(norm_xattn_g[l]), xa_wq[l].astype(BF16), kmem, vmem, xa_wo[l].astype(BF16), B, L, n_mem)
        h = _moe(h, row(norm_ffn_g[l]), moe_group_w[l], moe_group_b[l], moe_expert_w[l], moe_expert_b[l],
                 moe_w_gate[l].astype(BF16), moe_w_up[l].astype(BF16), moe_w_down[l].astype(BF16))
    return _final_norm(h, row(norm_final_g)).reshape(B, L, D)
```

```python
import functools
import math

import jax
import jax.numpy as jnp
from jax import lax
from jax.experimental import pallas as pl
from jax.experimental.pallas import tpu as pltpu

F32 = jnp.float32
BF16 = jnp.bfloat16

D_MODEL = 1024
W_GROUP = 256
EPS = 1e-6

S5_GROUP_CH = 16
S5_GROUPS = 16
S5_STATE = 64
S5_NS = S5_GROUPS * S5_STATE

GLA_HEADS = 4
GLA_DV = 64
GLA_DK = 32
GLA_RANK = 16
GLA_TAU = 16.0
GLA_CHUNK = 16
GLA_COLS_PAD = 896

RW_HEADS = 4
RW_N = 64
RW_W_RANK = 32
RW_A_RANK = 32
RW_G_RANK = 64
RW_LN_EPS = 64e-5
RW_COLS = 896
RW_CHUNK = 64

CONV_WIDTH = 31
CONV_LN_EPS = 1e-5
CONV_HALO = 32

X_HEADS = 4
X_HEAD_DIM = 256

N_GROUPS = 4
EXP_PER_GROUP = 8
N_EXPERTS = 32
D_EXPERT = 512
MOE_BLOCK = 256
ROUTE_ROWS = 40
ROUTE_TILE = 512

VMEM_LIMIT = 56 * 1024 * 1024


def _cparams(*sem):
    return pltpu.CompilerParams(dimension_semantics=sem, vmem_limit_bytes=VMEM_LIMIT)


_NN = (((1,), (0,)), ((), ()))
_NT = (((1,), (1,)), ((), ()))
_TN = (((0,), (0,)), ((), ()))
_BNN = (((2,), (1,)), ((0,), (0,)))


def _dg(a, b, dims=_NN):
    return lax.dot_general(a, b, dims, preferred_element_type=F32)


def _split2(a):
    hi = a.astype(BF16)
    lo = (a - hi.astype(F32)).astype(BF16)
    return hi, lo


def _split3(a):
    hi = a.astype(BF16)
    r1 = a - hi.astype(F32)
    mid = r1.astype(BF16)
    lo = (r1 - mid.astype(F32)).astype(BF16)
    return hi, mid, lo


def _dot1(a, b, dims=_NN):
    return _dg(a.astype(BF16), b.astype(BF16), dims)


def _dot3(a, b, dims=_NN):
    ah, al = _split2(a)
    bh, bl = _split2(b)
    return _dg(ah, bh, dims) + (_dg(ah, bl, dims) + _dg(al, bh, dims))


def _dot_exact_rhs(a, b_bf16, dims=_NN):
    ah, am, al = _split3(a)
    return _dg(ah, b_bf16, dims) + (_dg(am, b_bf16, dims) + _dg(al, b_bf16, dims))


def _dot_exact_lhs(a_bf16, b, dims=_NN):
    bh, bm, bl = _split3(b)
    return _dg(a_bf16, bh, dims) + (_dg(a_bf16, bm, dims) + _dg(a_bf16, bl, dims))


def _iota(shape, dim):
    return lax.broadcasted_iota(jnp.int32, shape, dim)


def _rms(x, g):
    return x * lax.rsqrt(jnp.mean(x * x, axis=-1, keepdims=True) + EPS) * g


def _sigmoid(x):
    return 1.0 / (1.0 + jnp.exp(-x))


def _softplus(x):
    return jnp.maximum(x, 0.0) + jnp.log1p(jnp.exp(-jnp.abs(x)))


def _shift_rows(x, s, fill=None):
    rolled = pltpu.roll(x, s, 0)
    rows = _iota(x.shape, 0)
    if fill is None:
        fill = jnp.zeros_like(x)
    return jnp.where(rows >= s, rolled, fill)


def _in_proj_kernel(x_ref, g_ref, w5_ref, wg_ref, wr_ref, wc_ref, o5_ref, og_ref, or_ref, oc_ref):
    xb = _rms(x_ref[...], g_ref[...]).astype(BF16)
    o5_ref[...] = _dg(xb, w5_ref[...])
    og_ref[...] = _dg(xb, wg_ref[...])
    or_ref[...] = _dg(xb, wr_ref[...])
    oc_ref[...] = _dg(xb, wc_ref[...])


def _in_proj(h, g, w5, wg, wr, wc, tm=512):
    T = h.shape[0]
    ws = (w5, wg, wr, wc)
    row = lambda i: (i, 0)
    fixed = lambda i: (0, 0)
    return pl.pallas_call(
        _in_proj_kernel,
        grid=(T // tm,),
        in_specs=[pl.BlockSpec((tm, D_MODEL), row), pl.BlockSpec((1, D_MODEL), fixed)]
        + [pl.BlockSpec(w.shape, fixed) for w in ws],
        out_specs=[pl.BlockSpec((tm, w.shape[1]), row) for w in ws],
        out_shape=[jax.ShapeDtypeStruct((T, w.shape[1]), F32) for w in ws],
        compiler_params=_cparams("parallel"),
        name="in_proj",
    )(h, g, *ws)


def _kv_proj_kernel(m_ref, g_ref, wk_ref, wv_ref, k_ref, v_ref):
    mb = _rms(m_ref[...], g_ref[...]).astype(BF16)
    k_ref[...] = _dg(mb, wk_ref[...]).astype(BF16)
    v_ref[...] = _dg(mb, wv_ref[...]).astype(BF16)


def _kv_proj(mem2d, g, wk, wv, tm=256):
    R = mem2d.shape[0]
    row = lambda i: (i, 0)
    fixed = lambda i: (0, 0)
    return pl.pallas_call(
        _kv_proj_kernel,
        grid=(R // tm,),
        in_specs=[pl.BlockSpec((tm, D_MODEL), row), pl.BlockSpec((1, D_MODEL), fixed),
                  pl.BlockSpec((D_MODEL, D_MODEL), fixed), pl.BlockSpec((D_MODEL, D_MODEL), fixed)],
        out_specs=[pl.BlockSpec((tm, D_MODEL), row)] * 2,
        out_shape=[jax.ShapeDtypeStruct((R, D_MODEL), BF16)] * 2,
        compiler_params=_cparams("parallel"),
        name="kv_proj",
    )(mem2d, g, wk, wv)


def _out_proj_kernel(y5_ref, yg_ref, yr_ref, yc_ref, beta_ref, w_ref, h_ref, o_ref):
    acc = h_ref[...]
    for i, y_ref in enumerate((y5_ref, yg_ref, yr_ref, yc_ref)):
        lo = i * W_GROUP
        yb = (y_ref[...] * beta_ref[:, lo:lo + W_GROUP]).astype(BF16)
        acc = acc + _dg(yb, w_ref[lo:lo + W_GROUP, :])
    o_ref[...] = acc


def _out_proj(ys, beta, w, h, tm=512):
    T = h.shape[0]
    row = lambda i: (i, 0)
    fixed = lambda i: (0, 0)
    return pl.pallas_call(
        _out_proj_kernel,
        grid=(T // tm,),
        in_specs=[pl.BlockSpec((tm, W_GROUP), row)] * 4
        + [pl.BlockSpec((1, D_MODEL), fixed), pl.BlockSpec((D_MODEL, D_MODEL), fixed),
           pl.BlockSpec((tm, D_MODEL), row)],
        out_specs=pl.BlockSpec((tm, D_MODEL), row),
        out_shape=jax.ShapeDtypeStruct((T, D_MODEL), F32),
        compiler_params=_cparams("parallel"),
        name="out_proj",
    )(*ys, beta, w, h)


def _final_norm_kernel(x_ref, g_ref, o_ref):
    o_ref[...] = _rms(x_ref[...], g_ref[...])


def _final_norm(h, g, tm=1024):
    T = h.shape[0]
    return pl.pallas_call(
        _final_norm_kernel,
        grid=(T // tm,),
        in_specs=[pl.BlockSpec((tm, D_MODEL), lambda i: (i, 0)), pl.BlockSpec((1, D_MODEL), lambda i: (0, 0))],
        out_specs=pl.BlockSpec((tm, D_MODEL), lambda i: (i, 0)),
        out_shape=jax.ShapeDtypeStruct((T, D_MODEL), F32),
        compiler_params=_cparams("parallel"),
        name="final_norm",
    )(h, g)


def _s5_kernel(u_ref, bbd_ref, cbd_ref, lam_ref, pw_ref, d_ref, gw_ref, gb_ref, o_ref, st_ref, carry_ref):
    n = S5_NS
    tl = u_ref.shape[0]

    @pl.when(pl.program_id(1) == 0)
    def _():
        carry_ref[...] = jnp.zeros_like(carry_ref)

    u = u_ref[...]
    st_ref[...] = _dg(u.astype(BF16), bbd_ref[...])

    steps = [(1, lam_ref[0:1, :], lam_ref[1:2, :]), (2, lam_ref[2:3, :], lam_ref[3:4, :]),
             (4, lam_ref[4:5, :], lam_ref[5:6, :])]
    pw_re = pw_ref[:, 0:n]
    pw_im = pw_ref[:, n:2 * n]

    def body(i, carry):
        c_re, c_im = carry
        r0 = pl.multiple_of(i * 8, 8)
        x_re = st_ref[pl.ds(r0, 8), 0:n]
        x_im = st_ref[pl.ds(r0, 8), n:2 * n]
        for s, l_re, l_im in steps:
            s_re = _shift_rows(x_re, s)
            s_im = _shift_rows(x_im, s)
            x_re, x_im = x_re + (l_re * s_re - l_im * s_im), x_im + (l_re * s_im + l_im * s_re)
        x_re, x_im = x_re + (pw_re * c_re - pw_im * c_im), x_im + (pw_re * c_im + pw_im * c_re)
        st_ref[pl.ds(r0, 8), 0:n] = x_re
        st_ref[pl.ds(r0, 8), n:2 * n] = x_im
        return (jnp.broadcast_to(x_re[7:8, :], (8, n)), jnp.broadcast_to(x_im[7:8, :], (8, n)))

    c_re, c_im = lax.fori_loop(0, tl // 8, body, (carry_ref[:, 0:n], carry_ref[:, n:2 * n]))
    carry_ref[:, 0:n] = c_re
    carry_ref[:, n:2 * n] = c_im

    y = _dg(st_ref[...].astype(BF16), cbd_ref[...]) + d_ref[...] * u
    y = 0.5 * y * (1.0 + jnp.tanh(math.sqrt(2.0 / math.pi) * (y + 0.044715 * (y * y * y))))
    gate = _dg(y.astype(BF16), gw_ref[...]) + gb_ref[...]
    o_ref[...] = y * _sigmoid(gate)


def _s5_mixer(p, prm, B, L, tl=256):
    bbd, cbd, lam_tab, pw_tab, d, gw, gb = prm
    fixed = lambda b, l: (0, 0)
    return pl.pallas_call(
        _s5_kernel,
        grid=(B, L // tl),
        in_specs=[pl.BlockSpec((None, tl, W_GROUP), lambda b, l: (b, l, 0)),
                  pl.BlockSpec(bbd.shape, fixed), pl.BlockSpec(cbd.shape, fixed),
                  pl.BlockSpec(lam_tab.shape, fixed), pl.BlockSpec(pw_tab.shape, fixed),
                  pl.BlockSpec(d.shape, fixed), pl.BlockSpec(gw.shape, fixed), pl.BlockSpec(gb.shape, fixed)],
        out_specs=pl.BlockSpec((None, tl, W_GROUP), lambda b, l: (b, l, 0)),
        out_shape=jax.ShapeDtypeStruct((B, L, W_GROUP), F32),
        scratch_shapes=[pltpu.VMEM((tl, 2 * S5_NS), F32), pltpu.VMEM((8, 2 * S5_NS), F32)],
        compiler_params=_cparams("parallel", "arbitrary"),
        name="s5_mixer",
    )(p.reshape(B, L, W_GROUP), bbd, cbd, lam_tab, pw_tab, d, gw, gb)


def _s5_params(lam_re, lam_im, b_re, b_im, c_re, c_im, d_skip, log_dt, glu_w, glu_b):
    G, N, P = S5_GROUPS, S5_STATE, S5_GROUP_CH
    dt = jnp.exp(log_dt)[:, None]
    mag = jnp.exp(lam_re * dt)
    lb_re = mag * jnp.cos(lam_im * dt)
    lb_im = mag * jnp.sin(lam_im * dt)
    den = lam_re * lam_re + lam_im * lam_im
    n_re, n_im = lb_re - 1.0, lb_im
    f_re = (n_re * lam_re + n_im * lam_im) / den
    f_im = (n_im * lam_re - n_re * lam_im) / den
    bb_re = f_re[:, :, None] * b_re - f_im[:, :, None] * b_im
    bb_im = f_re[:, :, None] * b_im + f_im[:, :, None] * b_re
    eye = jnp.eye(G, dtype=F32)
    bbd_re = jnp.einsum('gnp,gh->gphn', bb_re, eye).reshape(G * P, G * N)
    bbd_im = jnp.einsum('gnp,gh->gphn', bb_im, eye).reshape(G * P, G * N)
    bbd = jnp.concatenate([bbd_re, bbd_im], axis=1).astype(BF16)
    cbd_re = jnp.einsum('gpn,gh->gnhp', c_re, eye).reshape(G * N, G * P)
    cbd_im = jnp.einsum('gpn,gh->gnhp', -c_im, eye).reshape(G * N, G * P)
    cbd = jnp.concatenate([cbd_re, cbd_im], axis=0).astype(BF16)

    def cmul(a, b):
        return a[0] * b[0] - a[1] * b[1], a[0] * b[1] + a[1] * b[0]

    l1 = (lb_re.reshape(1, G * N), lb_im.reshape(1, G * N))
    l2 = cmul(l1, l1)
    l4 = cmul(l2, l2)
    zero = jnp.zeros_like(l1[0])
    lam_tab = jnp.concatenate([l1[0], l1[1], l2[0], l2[1], l4[0], l4[1], zero, zero], axis=0)
    pws = [l1]
    for _ in range(7):
        pws.append(cmul(pws[-1], l1))
    pw_tab = jnp.concatenate([jnp.concatenate([p[0] for p in pws], axis=0),
                              jnp.concatenate([p[1] for p in pws], axis=0)], axis=1)
    return (bbd, cbd, lam_tab, pw_tab, d_skip.reshape(1, W_GROUP), glu_w.astype(BF16),
            glu_b.reshape(1, W_GROUP))


def _gla_kernel(p_ref, wup_ref, bup_ref, tri_ref, sege_ref, segv_ref, ng_ref, o_ref,
                st_ref, b_ref, oacc_ref):
    C = GLA_CHUNK
    tl = p_ref.shape[0]

    @pl.when(pl.program_id(1) == 0)
    def _():
        st_ref[...] = jnp.zeros_like(st_ref)

    z = p_ref[:, 768:896]
    x = _dg(z.astype(BF16), wup_ref[...]) + bup_ref[...]
    log_alpha = (jnp.minimum(x, 0.0) - jnp.log1p(jnp.exp(-jnp.abs(x)))) * (1.0 / GLA_TAU)
    b_ref[...] = _dot_exact_lhs(tri_ref[...], log_alpha)

    row_i = _iota((C, 128), 0)
    bd_mask = (_iota((256, 128), 0) >> 6) == (_iota((256, 128), 1) >> 5)

    def chunk(c, _):
        r0 = pl.multiple_of(c * C, C)
        q = p_ref[pl.ds(r0, C), 0:128] * (GLA_DK ** -0.5)
        k = p_ref[pl.ds(r0, C), 128:256]
        v = p_ref[pl.ds(r0, C), 256:512]
        b = b_ref[pl.ds(r0, C), :]
        parts = []
        for j in range(C):
            e = jnp.exp(b - b[j:j + 1, :])
            parts.append(jnp.where(row_i >= j, (q * k[j:j + 1, :]) * e, 0.0).astype(BF16))
        t = jnp.concatenate(parts, axis=0)
        w = _dg(t, sege_ref[...])
        o = w[0:C, :] * v[0:1, :]
        for j in range(1, C):
            o = o + w[j * C:(j + 1) * C, :] * v[j:j + 1, :]
        st = st_ref[...]
        qe = (q * jnp.exp(b)).astype(BF16)
        o = o + _dg(qe, st.astype(BF16), _NT)
        b_last = b[C - 1:C, :]
        kt = (k * jnp.exp(b_last - b)).astype(BF16)
        upd = _dg(v.astype(BF16), kt, _TN)
        st_ref[...] = st * jnp.exp(b_last) + jnp.where(bd_mask, upd, 0.0)
        oacc_ref[pl.ds(r0, C), :] = o
        return 0

    lax.fori_loop(0, tl // C, chunk, 0)

    o = oacc_ref[...]
    ms = _dot_exact_rhs(o * o, segv_ref[...]) * (1.0 / GLA_DV)
    g = p_ref[:, 512:768]
    o_ref[...] = o * lax.rsqrt(ms + EPS) * ng_ref[...] * (g * _sigmoid(g))


def _gla_mixer(p, prm, B, L, tl=256):
    wup, bup, ng = prm
    C = GLA_CHUNK
    r = jnp.arange(tl)
    tri = ((r[:, None] // C == r[None, :] // C) & (r[None, :] <= r[:, None])).astype(BF16)
    sege = (jnp.arange(128)[:, None] // GLA_DK == jnp.arange(256)[None, :] // GLA_DV).astype(BF16)
    segv = (jnp.arange(256)[:, None] // GLA_DV == jnp.arange(256)[None, :] // GLA_DV).astype(BF16)
    fixed = lambda b, l: (0, 0)
    return pl.pallas_call(
        _gla_kernel,
        grid=(B, L // tl),
        in_specs=[pl.BlockSpec((None, tl, GLA_COLS_PAD), lambda b, l: (b, l, 0)),
                  pl.BlockSpec(wup.shape, fixed), pl.BlockSpec(bup.shape, fixed),
                  pl.BlockSpec(tri.shape, fixed), pl.BlockSpec(sege.shape, fixed),
                  pl.BlockSpec(segv.shape, fixed), pl.BlockSpec(ng.shape, fixed)],
        out_specs=pl.BlockSpec((None, tl, W_GROUP), lambda b, l: (b, l, 0)),
        out_shape=jax.ShapeDtypeStruct((B, L, W_GROUP), F32),
        scratch_shapes=[pltpu.VMEM((256, 128), F32), pltpu.VMEM((tl, 128), F32), pltpu.VMEM((tl, 256), F32)],
        compiler_params=_cparams("parallel", "arbitrary"),
        name="gla_mixer",
    )(p.reshape(B, L, GLA_COLS_PAD), wup, bup, tri, sege, segv, ng)


def _gla_params(w_up, b_up, norm_g):
    wup = jnp.zeros((128, 128), F32).at[:GLA_RANK, :].set(w_up).astype(BF16)
    return wup, b_up.reshape(1, 128), norm_g.reshape(1, W_GROUP)


def _rw_chunk(p, st_ref, prev_ref, o_ref, bi, prm):
    (mu, w0, w2, a0, a2, g2, kkg, ka, rk, lng, lnb, seg, tri) = prm
    C = RW_CHUNK
    W = W_GROUP
    prev_row = jnp.broadcast_to(prev_ref[bi, 0:1, :], p.shape)
    p_prev = _shift_rows(p, 1, prev_row)
    prev_ref[bi, 0:1, :] = p[C - 1:C, :]
    xm = p + (p_prev - p) * mu
    r = xm[:, 0:W]
    k = xm[:, W:2 * W]
    v = xm[:, 2 * W:3 * W]
    z = xm[:, 3 * W:3 * W + 128]

    w = -_softplus(-(w0 + _dg(jnp.tanh(z).astype(BF16), w2))) - 0.5
    logd = -jnp.exp(w)
    alr = _sigmoid(a0 + _dg(z.astype(BF16), a2))
    g = _dg(_sigmoid(z).astype(BF16), g2)
    lp = _dot_exact_lhs(tri, logd)
    yield

    kk = k * kkg
    kk = kk / jnp.maximum(jnp.sqrt(_dot1(kk * kk, seg)), 1e-12)
    k2 = k * (1.0 + (alr - 1.0) * ka)
    av = -kk
    bv = kk * alr
    bonus = _dot1(r * k2 * rk, seg) * v
    yield

    lp_c = lp[C - 1:C, :]
    e_neg = jnp.exp(-lp)
    e_rem = jnp.exp(lp_c - lp)
    ah = av * jnp.exp(lp - logd)
    rh = r * jnp.exp(lp)
    bh = bv * e_neg
    kh = k2 * e_neg
    b2 = bv * e_rem
    kc = k2 * e_rem
    p_c = jnp.exp(lp_c)

    lane_head = _iota((C, W), 1) >> 6
    hm = [lane_head == h for h in range(RW_HEADS)]
    zeros = jnp.zeros((C, W), F32)
    lhs = jnp.concatenate([jnp.where(m, ah, zeros) for m in hm] + [jnp.where(m, rh, zeros) for m in hm], axis=0)
    lhs = lhs.astype(BF16)
    xb = _dot1(lhs, bh, _NT).reshape(8, C, C)
    xk = _dot1(lhs, kh, _NT).reshape(8, C, C)
    yield

    ti = _iota((RW_HEADS, C, C), 1)
    tj = _iota((RW_HEADS, C, C), 2)
    strict = ti > tj
    incl = ti >= tj
    lm = jnp.where(strict, xb[0:4], 0.0)
    mak = jnp.where(strict, xk[0:4], 0.0)
    nrb = jnp.where(incl, xb[4:8], 0.0)
    nrk = jnp.where(incl, xk[4:8], 0.0)

    def bmm(a, b):
        return _dot1(a, b, _BNN)

    tinv = jnp.where(ti == tj, 1.0, 0.0) + lm
    lpow = lm
    for _ in range(5):
        lpow = bmm(lpow, lpow)
        yield
        tinv = tinv + bmm(tinv, lpow)
        yield

    def apply(m, x):
        full = _dot1(m.reshape(RW_HEADS * C, C), x).reshape(RW_HEADS, C, W)
        out = jnp.where(hm[0], full[0], zeros)
        for h in range(1, RW_HEADS):
            out = out + jnp.where(hm[h], full[h], zeros)
        return out

    tm = bmm(tinv, mak)
    a2h = apply(tinv, ah)
    nkv = apply(nrk, v)
    yield
    wv = apply(tm, v)
    r2 = rh + apply(nrb, a2h)
    bd = (_iota((W, W), 0) >> 6) == (_iota((W, W), 1) >> 6)
    eye = _iota((W, W), 0) == _iota((W, W), 1)
    a_c = jnp.where(eye, jnp.broadcast_to(p_c, (W, W)), 0.0) + jnp.where(bd, _dot1(b2, a2h, _TN), 0.0)
    yield
    y0 = apply(nrb, wv) + nkv
    g0 = jnp.where(bd, _dot1(b2, wv, _TN) + _dot1(kc, v, _TN), 0.0)
    s0 = st_ref[bi].astype(BF16)
    rs = _dot1(r2, s0)
    st_new = _dot1(a_c, s0)
    yield
    y = rs + y0
    st_ref[bi] = st_new + g0
    mean = _dot1(y, seg) * (1.0 / RW_N)
    yield
    yc = y - mean
    var = _dot1(yc * yc, seg) * (1.0 / RW_N)
    yield
    yn = yc * lax.rsqrt(var + RW_LN_EPS) * lng + lnb
    o_ref[bi] = (yn + bonus) * g


def _rw_kernel(p_ref, mu_ref, w0_ref, w2_ref, a0_ref, a2_ref, g2_ref, kk_ref, ka_ref, rk_ref,
               lng_ref, lnb_ref, seg_ref, tri_ref, o_ref, st_ref, prev_ref):
    @pl.when(pl.program_id(0) == 0)
    def _():
        st_ref[...] = jnp.zeros_like(st_ref)
        prev_ref[...] = jnp.zeros_like(prev_ref)

    prm = tuple(ref[...] for ref in (mu_ref, w0_ref, w2_ref, a0_ref, a2_ref, g2_ref, kk_ref, ka_ref, rk_ref,
                                     lng_ref, lnb_ref, seg_ref, tri_ref))
    chains = [_rw_chunk(p_ref[bi], st_ref, prev_ref, o_ref, bi, prm) for bi in range(p_ref.shape[0])]
    done = object()
    while chains:
        chains = [c for c in chains if next(c, done) is not done]


def _rw_mixer(p, prm, B, L):
    C = RW_CHUNK
    seg = (jnp.arange(256)[:, None] // RW_N == jnp.arange(256)[None, :] // RW_N).astype(BF16)
    tri = (jnp.arange(C)[None, :] <= jnp.arange(C)[:, None]).astype(BF16)
    fixed = lambda l: (0, 0)
    consts = tuple(prm) + (seg, tri)
    return pl.pallas_call(
        _rw_kernel,
        grid=(L // C,),
        in_specs=[pl.BlockSpec((B, C, RW_COLS), lambda l: (0, l, 0))]
        + [pl.BlockSpec(c.shape, fixed) for c in consts],
        out_specs=pl.BlockSpec((B, C, W_GROUP), lambda l: (0, l, 0)),
        out_shape=jax.ShapeDtypeStruct((B, L, W_GROUP), F32),
        scratch_shapes=[pltpu.VMEM((B, W_GROUP, W_GROUP), F32), pltpu.VMEM((B, 8, RW_COLS), F32)],
        compiler_params=_cparams("arbitrary"),
        name="rwkv7_mixer",
    )(p.reshape(B, L, RW_COLS), *consts)


def _rw_params(mu, w0, w2, a0, a2, g2, k_k, k_a, r_k, ln_g, ln_b):
    row = lambda t: t.reshape(1, -1)
    w2p = jnp.zeros((128, W_GROUP), F32).at[0:RW_W_RANK].set(w2).astype(BF16)
    a2p = jnp.zeros((128, W_GROUP), F32).at[RW_W_RANK:RW_W_RANK + RW_A_RANK].set(a2).astype(BF16)
    g2p = jnp.zeros((128, W_GROUP), F32).at[RW_W_RANK + RW_A_RANK:].set(g2).astype(BF16)
    return (row(mu), row(w0), w2p, row(a0), a2p, g2p, row(k_k), row(k_a), row(r_k), row(ln_g), row(ln_b))


def _conv_kernel(p_ref, w_ref, b_ref, lg_ref, lb_ref, o_ref, u_ref):
    tl = p_ref.shape[0]
    H = CONV_HALO
    RC = 64

    @pl.when(pl.program_id(1) == 0)
    def _():
        u_ref[0:H, :] = jnp.zeros((H, W_GROUP), F32)

    u_ref[H:H + tl, :] = p_ref[:, 0:W_GROUP] * _sigmoid(p_ref[:, W_GROUP:2 * W_GROUP])
    off = H - (CONV_WIDTH - 1)
    for c in range(tl // RC):
        acc = jnp.zeros((RC, W_GROUP), F32)
        for j in range(CONV_WIDTH):
            acc = acc + w_ref[j:j + 1, :] * u_ref[c * RC + off + j:c * RC + off + j + RC, :]
        y = acc + b_ref[...]
        mu = jnp.mean(y, axis=-1, keepdims=True)
        yc = y - mu
        yn = yc * lax.rsqrt(jnp.mean(yc * yc, axis=-1, keepdims=True) + CONV_LN_EPS) * lg_ref[...] + lb_ref[...]
        o_ref[c * RC:(c + 1) * RC, :] = yn * _sigmoid(yn)
    tail = u_ref[tl:tl + H, :]
    u_ref[0:H, :] = tail


def _conv_mixer(p, prm, B, L, tl=512):
    w, b, lg, lb = prm
    fixed = lambda b_, l: (0, 0)
    return pl.pallas_call(
        _conv_kernel,
        grid=(B, L // tl),
        in_specs=[pl.BlockSpec((None, tl, 2 * W_GROUP), lambda b_, l: (b_, l, 0)),
                  pl.BlockSpec(w.shape, fixed), pl.BlockSpec(b.shape, fixed),
                  pl.BlockSpec(lg.shape, fixed), pl.BlockSpec(lb.shape, fixed)],
        out_specs=pl.BlockSpec((None, tl, W_GROUP), lambda b_, l: (b_, l, 0)),
        out_shape=jax.ShapeDtypeStruct((B, L, W_GROUP), F32),
        scratch_shapes=[pltpu.VMEM((tl + CONV_HALO, W_GROUP), F32)],
        compiler_params=_cparams("parallel", "arbitrary"),
        name="conv_mixer",
    )(p.reshape(B, L, 2 * W_GROUP), w, b, lg, lb)


def _xattn_kernel(h_ref, g_ref, wq_ref, k_ref, v_ref, wo_ref, o_ref):
    x = h_ref[...]
    q = _dg(_rms(x, g_ref[...]).astype(BF16), wq_ref[...])
    acc = x
    for hd in range(X_HEADS):
        lo = hd * X_HEAD_DIM
        s = _dg(q[:, lo:lo + X_HEAD_DIM].astype(BF16), k_ref[:, lo:lo + X_HEAD_DIM], _NT) * (X_HEAD_DIM ** -0.5)
        e = jnp.exp(s - jnp.max(s, axis=-1, keepdims=True))
        pr = e / jnp.sum(e, axis=-1, keepdims=True)
        oh = _dg(pr.astype(BF16), v_ref[:, lo:lo + X_HEAD_DIM])
        acc = acc + _dg(oh.astype(BF16), wo_ref[lo:lo + X_HEAD_DIM, :])
    o_ref[...] = acc


def _xattn(h, g, wq, k, v, wo, B, L, n_mem, tq=512):
    fixed = lambda b, l: (0, 0)
    out = pl.pallas_call(
        _xattn_kernel,
        grid=(B, L // tq),
        in_specs=[pl.BlockSpec((None, tq, D_MODEL), lambda b, l: (b, l, 0)),
                  pl.BlockSpec((1, D_MODEL), fixed), pl.BlockSpec((D_MODEL, D_MODEL), fixed),
                  pl.BlockSpec((None, n_mem, D_MODEL), lambda b, l: (b, 0, 0)),
                  pl.BlockSpec((None, n_mem, D_MODEL), lambda b, l: (b, 0, 0)),
                  pl.BlockSpec((D_MODEL, D_MODEL), fixed)],
        out_specs=pl.BlockSpec((None, tq, D_MODEL), lambda b, l: (b, l, 0)),
        out_shape=jax.ShapeDtypeStruct((B, L, D_MODEL), F32),
        compiler_params=_cparams("parallel", "parallel"),
        name="mem_xattn",
    )(h.reshape(B, L, D_MODEL), g, wq, k.reshape(B, n_mem, D_MODEL), v.reshape(B, n_mem, D_MODEL), wo)
    return out.reshape(B * L, D_MODEL)


def _router_kernel(h_ref, g_ref, wr_ref, br_ref, xn_ref, eid_ref, gate_ref):
    xn = _rms(h_ref[...], g_ref[...])
    xn_ref[...] = xn
    lg = _dot3(wr_ref[...], xn, _NT) + br_ref[...]
    tm = lg.shape[1]
    gl = lg[0:8, :]
    sub = _iota((8, tm), 0)
    gmax = jnp.max(gl, axis=0, keepdims=True)
    gsel = jnp.min(jnp.where(gl == gmax, sub, 8), axis=0, keepdims=True)
    g_w = 1.0 / jnp.sum(jnp.exp(gl - gmax), axis=0, keepdims=True)
    e_in = jnp.zeros((8, tm), F32)
    for grp in range(N_GROUPS):
        e_in = e_in + jnp.where(gsel == grp, lg[8 + 8 * grp:16 + 8 * grp, :], 0.0)
    t1 = jnp.max(e_in, axis=0, keepdims=True)
    i1 = jnp.min(jnp.where(e_in == t1, sub, 8), axis=0, keepdims=True)
    rest = jnp.where(sub == i1, -jnp.inf, e_in)
    t2 = jnp.max(rest, axis=0, keepdims=True)
    i2 = jnp.min(jnp.where(rest == t2, sub, 8), axis=0, keepdims=True)
    e2 = jnp.exp(t2 - t1)
    p1 = 1.0 / (1.0 + e2)
    eid_ref[0:1, :] = gsel * EXP_PER_GROUP + i1
    eid_ref[1:2, :] = gsel * EXP_PER_GROUP + i2
    gate_ref[0:1, :] = p1 * g_w
    gate_ref[1:2, :] = (e2 * p1) * g_w


def _router(h, g, wr, br, tm=ROUTE_TILE):
    T = h.shape[0]
    nt = T // tm
    return pl.pallas_call(
        _router_kernel,
        grid=(nt,),
        in_specs=[pl.BlockSpec((tm, D_MODEL), lambda i: (i, 0)), pl.BlockSpec((1, D_MODEL), lambda i: (0, 0)),
                  pl.BlockSpec(wr.shape, lambda i: (0, 0)), pl.BlockSpec(br.shape, lambda i: (0, 0))],
        out_specs=[pl.BlockSpec((tm, D_MODEL), lambda i: (i, 0)),
                   pl.BlockSpec((None, 2, tm), lambda i: (i, 0, 0)),
                   pl.BlockSpec((None, 2, tm), lambda i: (i, 0, 0))],
        out_shape=[jax.ShapeDtypeStruct((T, D_MODEL), F32),
                   jax.ShapeDtypeStruct((nt, 2, tm), jnp.int32),
                   jax.ShapeDtypeStruct((nt, 2, tm), F32)],
        compiler_params=_cparams("parallel"),
        name="moe_router",
    )(h, g, wr, br)


def _rank_kernel(eid_ref, ut_ref, rank_ref, cnt_ref, carry_ref):
    tm = eid_ref.shape[1]

    @pl.when(pl.program_id(0) == 0)
    def _():
        carry_ref[...] = jnp.zeros_like(carry_ref)

    ex = _iota((N_EXPERTS, tm), 0)
    oh0 = jnp.where(ex == eid_ref[0:1, :], 1.0, 0.0)
    oh1 = jnp.where(ex == eid_ref[1:2, :], 1.0, 0.0)
    oh = oh0 + oh1
    cum = _dg(oh.astype(BF16), ut_ref[...])
    carry = carry_ref[:, 0:1]
    before = cum - oh + carry
    rank_ref[0:1, :] = jnp.sum(oh0 * before, axis=0, keepdims=True).astype(jnp.int32)
    rank_ref[1:2, :] = jnp.sum(oh1 * before, axis=0, keepdims=True).astype(jnp.int32)
    total = carry + cum[:, tm - 1:tm]
    carry_ref[...] = jnp.broadcast_to(total, carry_ref.shape)
    cnt_ref[...] = jnp.broadcast_to(total, cnt_ref.shape).astype(jnp.int32)


def _rank(eid, tm=ROUTE_TILE):
    nt = eid.shape[0]
    ut = (jnp.arange(tm)[:, None] <= jnp.arange(tm)[None, :]).astype(BF16)
    return pl.pallas_call(
        _rank_kernel,
        grid=(nt,),
        in_specs=[pl.BlockSpec((None, 2, tm), lambda i: (i, 0, 0)), pl.BlockSpec((tm, tm), lambda i: (0, 0))],
        out_specs=[pl.BlockSpec((None, 2, tm), lambda i: (i, 0, 0)),
                   pl.BlockSpec((N_EXPERTS, 128), lambda i: (0, 0))],
        out_shape=[jax.ShapeDtypeStruct((nt, 2, tm), jnp.int32),
                   jax.ShapeDtypeStruct((N_EXPERTS, 128), jnp.int32)],
        scratch_shapes=[pltpu.VMEM((N_EXPERTS, 128), F32)],
        compiler_params=_cparams("arbitrary"),
        name="moe_rank",
    )(eid, ut)


def _slot_kernel(ps_ref, eid_ref, rank_ref, slot_ref):
    eid = eid_ref[...]
    acc = rank_ref[...]
    for e in range(N_EXPERTS):
        acc = acc + jnp.where(eid == e, ps_ref[e], 0)
    slot_ref[...] = acc


def _slots(pad_starts, eid, rank):
    nt, _, tm = eid.shape
    shape2d = (nt * 2, tm)
    out = pl.pallas_call(
        _slot_kernel,
        grid_spec=pltpu.PrefetchScalarGridSpec(
            num_scalar_prefetch=1,
            grid=(1,),
            in_specs=[pl.BlockSpec(shape2d, lambda i, ps: (0, 0))] * 2,
            out_specs=pl.BlockSpec(shape2d, lambda i, ps: (0, 0)),
        ),
        out_shape=jax.ShapeDtypeStruct(shape2d, jnp.int32),
        compiler_params=_cparams("arbitrary"),
        name="moe_slots",
    )(pad_starts, eid.reshape(shape2d), rank.reshape(shape2d))
    return out.reshape(nt, 2, tm)


ROW_DMA_UNROLL = 8


def _dispatch_kernel(slot_ref, x_ref, xs_in_ref, xs_ref, sem):
    del xs_in_ref
    tm = x_ref.shape[0]

    def issue(i, _):
        for u in range(ROW_DMA_UNROLL):
            r = i * ROW_DMA_UNROLL + u
            for k in range(2):
                pltpu.make_async_copy(x_ref.at[pl.ds(r, 1)], xs_ref.at[pl.ds(slot_ref[k, r], 1)], sem).start()
        return 0

    lax.fori_loop(0, tm // ROW_DMA_UNROLL, issue, 0)
    for _ in range(2):
        pltpu.make_async_copy(x_ref, xs_ref.at[pl.ds(0, tm)], sem).wait()


def _dispatch(slot, xn, xs_zero, tm=ROUTE_TILE):
    T = xn.shape[0]
    return pl.pallas_call(
        _dispatch_kernel,
        grid=(T // tm,),
        in_specs=[pl.BlockSpec((None, 2, tm), lambda i: (i, 0, 0), memory_space=pltpu.SMEM),
                  pl.BlockSpec((tm, D_MODEL), lambda i: (i, 0)),
                  pl.BlockSpec(memory_space=pl.ANY)],
        out_specs=pl.BlockSpec(memory_space=pl.ANY),
        scratch_shapes=[pltpu.SemaphoreType.DMA(())],
        out_shape=jax.ShapeDtypeStruct(xs_zero.shape, F32),
        input_output_aliases={2: 0},
        compiler_params=_cparams("arbitrary"),
        name="moe_dispatch",
    )(slot, xn, xs_zero)


def _expert_kernel(be_ref, x_ref, wg_ref, wu_ref, wd_ref, y_ref):
    del be_ref
    xb = x_ref[...].astype(BF16)
    gate = _dg(xb, wg_ref[...])
    hid = (gate * _sigmoid(gate)) * _dg(xb, wu_ref[...])
    y_ref[...] = _dg(hid.astype(BF16), wd_ref[...])


def _experts(block_exp, xs, w_gate, w_up, w_down):
    n_slots = xs.shape[0]
    nb = n_slots // MOE_BLOCK
    return pl.pallas_call(
        _expert_kernel,
        grid_spec=pltpu.PrefetchScalarGridSpec(
            num_scalar_prefetch=1,
            grid=(nb,),
            in_specs=[pl.BlockSpec((MOE_BLOCK, D_MODEL), lambda i, be: (i, 0)),
                      pl.BlockSpec((None, D_MODEL, D_EXPERT), lambda i, be: (be[i], 0, 0)),
                      pl.BlockSpec((None, D_MODEL, D_EXPERT), lambda i, be: (be[i], 0, 0)),
                      pl.BlockSpec((None, D_EXPERT, D_MODEL), lambda i, be: (be[i], 0, 0))],
            out_specs=pl.BlockSpec((MOE_BLOCK, D_MODEL), lambda i, be: (i, 0)),
        ),
        out_shape=jax.ShapeDtypeStruct((n_slots, D_MODEL), F32),
        compiler_params=_cparams("arbitrary"),
        name="moe_experts",
    )(block_exp, xs, w_gate, w_up, w_down)


def _combine_kernel(slot_ref, gate_ref, h_ref, ys_ref, o_ref, y0_ref, y1_ref, sem):
    tm = h_ref.shape[0]
    bufs = (y0_ref, y1_ref)

    def issue(i, _):
        for u in range(ROW_DMA_UNROLL):
            r = i * ROW_DMA_UNROLL + u
            for k in range(2):
                pltpu.make_async_copy(ys_ref.at[pl.ds(slot_ref[k, r], 1)], bufs[k].at[pl.ds(r, 1)], sem).start()
        return 0

    lax.fori_loop(0, tm // ROW_DMA_UNROLL, issue, 0)
    for k in range(2):
        pltpu.make_async_copy(ys_ref.at[pl.ds(0, tm)], bufs[k], sem).wait()
    o_ref[...] = h_ref[...] + gate_ref[:, 0:1] * y0_ref[...] + gate_ref[:, 1:2] * y1_ref[...]


def _combine(slot, gate_t, h, ys, tm=ROUTE_TILE):
    T = h.shape[0]
    return pl.pallas_call(
        _combine_kernel,
        grid=(T // tm,),
        in_specs=[pl.BlockSpec((None, 2, tm), lambda i: (i, 0, 0), memory_space=pltpu.SMEM),
                  pl.BlockSpec((tm, 2), lambda i: (i, 0)),
                  pl.BlockSpec((tm, D_MODEL), lambda i: (i, 0)),
                  pl.BlockSpec(memory_space=pl.ANY)],
        out_specs=pl.BlockSpec((tm, D_MODEL), lambda i: (i, 0)),
        scratch_shapes=[pltpu.VMEM((tm, D_MODEL), F32), pltpu.VMEM((tm, D_MODEL), F32),
                        pltpu.SemaphoreType.DMA(())],
        out_shape=jax.ShapeDtypeStruct((T, D_MODEL), F32),
        compiler_params=_cparams("arbitrary"),
        name="moe_combine",
    )(slot, gate_t, h, ys)


def _moe(h, g, group_w, group_b, expert_w, expert_b, w_gate, w_up, w_down):
    T = h.shape[0]
    wr = jnp.zeros((ROUTE_ROWS, D_MODEL), F32).at[0:N_GROUPS].set(group_w.T).at[8:].set(expert_w.T)
    br = jnp.full((ROUTE_ROWS, 1), -1e30, F32).at[0:N_GROUPS, 0].set(group_b).at[8:, 0].set(expert_b)
    xn, eid, gate = _router(h, g, wr, br)
    rank, cnt = _rank(eid)
    counts = cnt[:, 0]
    padded = (counts + MOE_BLOCK - 1) // MOE_BLOCK * MOE_BLOCK
    pad_ends = jnp.cumsum(padded)
    pad_starts = (pad_ends - padded).astype(jnp.int32)
    n_slots = ((T * 2 + MOE_BLOCK - 1) // MOE_BLOCK + N_EXPERTS) * MOE_BLOCK
    block_start = jnp.arange(n_slots // MOE_BLOCK, dtype=jnp.int32) * MOE_BLOCK
    owner = jnp.sum((pad_ends[None, :] <= block_start[:, None]).astype(jnp.int32), axis=1)
    block_exp = jnp.minimum(owner, N_EXPERTS - 1).astype(jnp.int32)
    slot = _slots(pad_starts, eid, rank)
    xs = _dispatch(slot, xn, jnp.zeros((n_slots, D_MODEL), F32))
    ys = _experts(block_exp, xs, w_gate, w_up, w_down)
    gate_t = gate.transpose(0, 2, 1).reshape(T, 2)
    return _combine(slot, gate_t, h, ys)


def kernel(x, mem, norm_mix_g, w_in, w_out, mix_beta, s5_lam_re, s5_lam_im, s5_b_re, s5_b_im, s5_c_re, s5_c_im, s5_d, s5_log_dt, s5_glu_w, s5_glu_b, gla_w_up, gla_b_up, gla_norm_g, rw_mu, rw_w0, rw_w2, rw_a0, rw_a2, rw_g2, rw_k_k, rw_k_a, rw_r_k, rw_ln_g, rw_ln_b, conv_w, conv_b, conv_ln_g, conv_ln_b, norm_xattn_g, norm_mem_g, xa_wq, xa_wk, xa_wv, xa_wo, norm_ffn_g, moe_group_w, moe_group_b, moe_expert_w, moe_expert_b, moe_w_gate, moe_w_up, moe_w_down, norm_final_g):
    B, L, D = x.shape
    n_mem = mem.shape[1]
    depth = w_in.shape[0]
    T = B * L
    h = x.reshape(T, D)
    mem2d = mem.reshape(B * n_mem, D)
    row = lambda t: t.reshape(1, -1)
    c0 = W_GROUP
    c1 = c0 + 784
    c2 = c1 + RW_COLS
    for l in range(depth):
        wl = w_in[l]
        w5 = wl[:, :c0].astype(BF16)
        wg = jnp.pad(wl[:, c0:c1], ((0, 0), (0, GLA_COLS_PAD - 784))).astype(BF16)
        wr = wl[:, c1:c2].astype(BF16)
        wc = wl[:, c2:].astype(BF16)
        p5, pg, pr, pc = _in_proj(h, row(norm_mix_g[l]), w5, wg, wr, wc)
        y5 = _s5_mixer(p5, _s5_params(s5_lam_re[l], s5_lam_im[l], s5_b_re[l], s5_b_im[l], s5_c_re[l], s5_c_im[l],
                                      s5_d[l], s5_log_dt[l], s5_glu_w[l], s5_glu_b[l]), B, L)
        yg = _gla_mixer(pg, _gla_params(gla_w_up[l], gla_b_up[l], gla_norm_g[l]), B, L)
        yr = _rw_mixer(pr, _rw_params(rw_mu[l], rw_w0[l], rw_w2[l], rw_a0[l], rw_a2[l], rw_g2[l], rw_k_k[l],
                                      rw_k_a[l], rw_r_k[l], rw_ln_g[l], rw_ln_b[l]), B, L)
        yc = _conv_mixer(pc, (conv_w[l], row(conv_b[l]), row(conv_ln_g[l]), row(conv_ln_b[l])), B, L)
        ys = [t.reshape(T, W_GROUP) for t in (y5, yg, yr, yc)]
        h = _out_proj(ys, row(mix_beta[l]), w_out[l].astype(BF16), h)
        kmem, vmem = _kv_proj(mem2d, row(norm_mem_g[l]), xa_wk[l].astype(BF16), xa_wv[l].astype(BF16))
        h = _xattn(h, row(norm_xattn_g[l]), xa_wq[l].astype(BF16), kmem, vmem, xa_wo[l].astype(BF16), B, L, n_mem)
        h = _moe(h, row(norm_ffn_g[l]), moe_group_w[l], moe_group_b[l], moe_expert_w[l], moe_expert_b[l],
                 moe_w_gate[l].astype(BF16), moe_w_up[l].astype(BF16), moe_w_down[l].astype(BF16))
    return _final_norm(h, row(norm_final_g)).reshape(B, L, D)
```

```python
import functools
import math

import jax
import jax.numpy as jnp
from jax import lax
from jax.experimental import pallas as pl
from jax.experimental.pallas import tpu as pltpu

F32 = jnp.float32
BF16 = jnp.bfloat16

D_MODEL = 1024
W_GROUP = 256
EPS = 1e-6

S5_GROUP_CH = 16
S5_GROUPS = 16
S5_STATE = 64
S5_NS = S5_GROUPS * S5_STATE

GLA_HEADS = 4
GLA_DV = 64
GLA_DK = 32
GLA_RANK = 16
GLA_TAU = 16.0
GLA_CHUNK = 16
GLA_COLS_PAD = 896

RW_HEADS = 4
RW_N = 64
RW_W_RANK = 32
RW_A_RANK = 32
RW_G_RANK = 64
RW_LN_EPS = 64e-5
RW_COLS = 896
RW_CHUNK = 64

CONV_WIDTH = 31
CONV_LN_EPS = 1e-5
CONV_HALO = 32

X_HEADS = 4
X_HEAD_DIM = 256

N_GROUPS = 4
EXP_PER_GROUP = 8
N_EXPERTS = 32
D_EXPERT = 512
MOE_BLOCK = 256
ROUTE_ROWS = 40
ROUTE_TILE = 512

SUBLANES = 8
VMEM_LIMIT = 56 * 1024 * 1024


def _cparams(*sem):
    return pltpu.CompilerParams(dimension_semantics=sem, vmem_limit_bytes=VMEM_LIMIT)


_NN = (((1,), (0,)), ((), ()))
_NT = (((1,), (1,)), ((), ()))
_TN = (((0,), (0,)), ((), ()))
_BNN = (((2,), (1,)), ((0,), (0,)))


def _dg(a, b, dims=_NN):
    return lax.dot_general(a, b, dims, preferred_element_type=F32)


def _split2(a):
    hi = a.astype(BF16)
    lo = (a - hi.astype(F32)).astype(BF16)
    return hi, lo


def _split3(a):
    hi = a.astype(BF16)
    r1 = a - hi.astype(F32)
    mid = r1.astype(BF16)
    lo = (r1 - mid.astype(F32)).astype(BF16)
    return hi, mid, lo


def _dot1(a, b, dims=_NN):
    return _dg(a.astype(BF16), b.astype(BF16), dims)


def _dot3(a, b, dims=_NN):
    ah, al = _split2(a)
    bh, bl = _split2(b)
    return _dg(ah, bh, dims) + (_dg(ah, bl, dims) + _dg(al, bh, dims))


def _dot_exact_rhs(a, b_bf16, dims=_NN):
    ah, am, al = _split3(a)
    return _dg(ah, b_bf16, dims) + (_dg(am, b_bf16, dims) + _dg(al, b_bf16, dims))


def _dot_exact_lhs(a_bf16, b, dims=_NN):
    bh, bm, bl = _split3(b)
    return _dg(a_bf16, bh, dims) + (_dg(a_bf16, bm, dims) + _dg(a_bf16, bl, dims))


def _iota(shape, dim):
    return lax.broadcasted_iota(jnp.int32, shape, dim)


def _rms(x, g):
    return x * lax.rsqrt(jnp.mean(x * x, axis=-1, keepdims=True) + EPS) * g


def _sigmoid(x):
    return 1.0 / (1.0 + jnp.exp(-x))


def _softplus(x):
    return jnp.maximum(x, 0.0) + jnp.log1p(jnp.exp(-jnp.abs(x)))


def _round_robin(chains):
    done = object()
    while chains:
        chains = [c for c in chains if next(c, done) is not done]


def _shift_rows(x, s, fill=None):
    rolled = pltpu.roll(x, s, 0)
    rows = _iota(x.shape, 0)
    if fill is None:
        fill = jnp.zeros_like(x)
    return jnp.where(rows >= s, rolled, fill)


def _in_proj_kernel(x_ref, g_ref, w5_ref, wg_ref, wr_ref, wc_ref, o5_ref, og_ref, or_ref, oc_ref):
    xb = _rms(x_ref[...], g_ref[...]).astype(BF16)
    o5_ref[...] = _dg(xb, w5_ref[...])
    og_ref[...] = _dg(xb, wg_ref[...])
    or_ref[...] = _dg(xb, wr_ref[...])
    oc_ref[...] = _dg(xb, wc_ref[...])


def _in_proj(h, g, w5, wg, wr, wc, tm=512):
    T = h.shape[0]
    ws = (w5, wg, wr, wc)
    row = lambda i: (i, 0)
    fixed = lambda i: (0, 0)
    return pl.pallas_call(
        _in_proj_kernel,
        grid=(T // tm,),
        in_specs=[pl.BlockSpec((tm, D_MODEL), row), pl.BlockSpec((1, D_MODEL), fixed)]
        + [pl.BlockSpec(w.shape, fixed) for w in ws],
        out_specs=[pl.BlockSpec((tm, w.shape[1]), row) for w in ws],
        out_shape=[jax.ShapeDtypeStruct((T, w.shape[1]), F32) for w in ws],
        compiler_params=_cparams("parallel"),
        name="in_proj",
    )(h, g, *ws)


def _kv_proj_kernel(m_ref, g_ref, wk_ref, wv_ref, k_ref, v_ref):
    mb = _rms(m_ref[...], g_ref[...]).astype(BF16)
    k_ref[...] = _dg(mb, wk_ref[...]).astype(BF16)
    v_ref[...] = _dg(mb, wv_ref[...]).astype(BF16)


def _kv_proj(mem2d, g, wk, wv, tm=256):
    R = mem2d.shape[0]
    row = lambda i: (i, 0)
    fixed = lambda i: (0, 0)
    return pl.pallas_call(
        _kv_proj_kernel,
        grid=(R // tm,),
        in_specs=[pl.BlockSpec((tm, D_MODEL), row), pl.BlockSpec((1, D_MODEL), fixed),
                  pl.BlockSpec((D_MODEL, D_MODEL), fixed), pl.BlockSpec((D_MODEL, D_MODEL), fixed)],
        out_specs=[pl.BlockSpec((tm, D_MODEL), row)] * 2,
        out_shape=[jax.ShapeDtypeStruct((R, D_MODEL), BF16)] * 2,
        compiler_params=_cparams("parallel"),
        name="kv_proj",
    )(mem2d, g, wk, wv)


def _out_proj_kernel(y5_ref, yg_ref, yr_ref, yc_ref, beta_ref, w_ref, h_ref, o_ref):
    acc = h_ref[...]
    for i, y_ref in enumerate((y5_ref, yg_ref, yr_ref, yc_ref)):
        lo = i * W_GROUP
        yb = (y_ref[...] * beta_ref[:, lo:lo + W_GROUP]).astype(BF16)
        acc = acc + _dg(yb, w_ref[lo:lo + W_GROUP, :])
    o_ref[...] = acc


def _out_proj(ys, beta, w, h, tm=512):
    T = h.shape[0]
    row = lambda i: (i, 0)
    fixed = lambda i: (0, 0)
    return pl.pallas_call(
        _out_proj_kernel,
        grid=(T // tm,),
        in_specs=[pl.BlockSpec((tm, W_GROUP), row)] * 4
        + [pl.BlockSpec((1, D_MODEL), fixed), pl.BlockSpec((D_MODEL, D_MODEL), fixed),
           pl.BlockSpec((tm, D_MODEL), row)],
        out_specs=pl.BlockSpec((tm, D_MODEL), row),
        out_shape=jax.ShapeDtypeStruct((T, D_MODEL), F32),
        compiler_params=_cparams("parallel"),
        name="out_proj",
    )(*ys, beta, w, h)


def _final_norm_kernel(x_ref, g_ref, o_ref):
    o_ref[...] = _rms(x_ref[...], g_ref[...])


def _final_norm(h, g, tm=1024):
    T = h.shape[0]
    return pl.pallas_call(
        _final_norm_kernel,
        grid=(T // tm,),
        in_specs=[pl.BlockSpec((tm, D_MODEL), lambda i: (i, 0)), pl.BlockSpec((1, D_MODEL), lambda i: (0, 0))],
        out_specs=pl.BlockSpec((tm, D_MODEL), lambda i: (i, 0)),
        out_shape=jax.ShapeDtypeStruct((T, D_MODEL), F32),
        compiler_params=_cparams("parallel"),
        name="final_norm",
    )(h, g)


def _s5_kernel(u_ref, bbd_ref, cbd_ref, lam_ref, pw_ref, d_ref, gw_ref, gb_ref, o_ref, st_ref, carry_ref):
    n = S5_NS
    tl = u_ref.shape[0]

    @pl.when(pl.program_id(1) == 0)
    def _():
        carry_ref[...] = jnp.zeros_like(carry_ref)

    u = u_ref[...]
    st_ref[...] = _dg(u.astype(BF16), bbd_ref[...])

    steps = [(1, lam_ref[0:1, :], lam_ref[1:2, :]), (2, lam_ref[2:3, :], lam_ref[3:4, :]),
             (4, lam_ref[4:5, :], lam_ref[5:6, :])]
    pw_re = pw_ref[:, 0:n]
    pw_im = pw_ref[:, n:2 * n]

    def body(i, carry):
        c_re, c_im = carry
        r0 = pl.multiple_of(i * 8, 8)
        x_re = st_ref[pl.ds(r0, 8), 0:n]
        x_im = st_ref[pl.ds(r0, 8), n:2 * n]
        for s, l_re, l_im in steps:
            s_re = _shift_rows(x_re, s)
            s_im = _shift_rows(x_im, s)
            x_re, x_im = x_re + (l_re * s_re - l_im * s_im), x_im + (l_re * s_im + l_im * s_re)
        x_re, x_im = x_re + (pw_re * c_re - pw_im * c_im), x_im + (pw_re * c_im + pw_im * c_re)
        st_ref[pl.ds(r0, 8), 0:n] = x_re
        st_ref[pl.ds(r0, 8), n:2 * n] = x_im
        return (jnp.broadcast_to(x_re[7:8, :], (8, n)), jnp.broadcast_to(x_im[7:8, :], (8, n)))

    c_re, c_im = lax.fori_loop(0, tl // 8, body, (carry_ref[:, 0:n], carry_ref[:, n:2 * n]))
    carry_ref[:, 0:n] = c_re
    carry_ref[:, n:2 * n] = c_im

    y = _dg(st_ref[...].astype(BF16), cbd_ref[...]) + d_ref[...] * u
    y = 0.5 * y * (1.0 + jnp.tanh(math.sqrt(2.0 / math.pi) * (y + 0.044715 * (y * y * y))))
    gate = _dg(y.astype(BF16), gw_ref[...]) + gb_ref[...]
    o_ref[...] = y * _sigmoid(gate)


def _s5_mixer(p, prm, B, L, tl=256):
    bbd, cbd, lam_tab, pw_tab, d, gw, gb = prm
    fixed = lambda b, l: (0, 0)
    return pl.pallas_call(
        _s5_kernel,
        grid=(B, L // tl),
        in_specs=[pl.BlockSpec((None, tl, W_GROUP), lambda b, l: (b, l, 0)),
                  pl.BlockSpec(bbd.shape, fixed), pl.BlockSpec(cbd.shape, fixed),
                  pl.BlockSpec(lam_tab.shape, fixed), pl.BlockSpec(pw_tab.shape, fixed),
                  pl.BlockSpec(d.shape, fixed), pl.BlockSpec(gw.shape, fixed), pl.BlockSpec(gb.shape, fixed)],
        out_specs=pl.BlockSpec((None, tl, W_GROUP), lambda b, l: (b, l, 0)),
        out_shape=jax.ShapeDtypeStruct((B, L, W_GROUP), F32),
        scratch_shapes=[pltpu.VMEM((tl, 2 * S5_NS), F32), pltpu.VMEM((8, 2 * S5_NS), F32)],
        compiler_params=_cparams("parallel", "arbitrary"),
        name="s5_mixer",
    )(p.reshape(B, L, W_GROUP), bbd, cbd, lam_tab, pw_tab, d, gw, gb)


def _s5_params(lam_re, lam_im, b_re, b_im, c_re, c_im, d_skip, log_dt, glu_w, glu_b):
    G, N, P = S5_GROUPS, S5_STATE, S5_GROUP_CH
    dt = jnp.exp(log_dt)[:, None]
    mag = jnp.exp(lam_re * dt)
    lb_re = mag * jnp.cos(lam_im * dt)
    lb_im = mag * jnp.sin(lam_im * dt)
    den = lam_re * lam_re + lam_im * lam_im
    n_re, n_im = lb_re - 1.0, lb_im
    f_re = (n_re * lam_re + n_im * lam_im) / den
    f_im = (n_im * lam_re - n_re * lam_im) / den
    bb_re = f_re[:, :, None] * b_re - f_im[:, :, None] * b_im
    bb_im = f_re[:, :, None] * b_im + f_im[:, :, None] * b_re
    eye = jnp.eye(G, dtype=F32)
    bbd_re = jnp.einsum('gnp,gh->gphn', bb_re, eye).reshape(G * P, G * N)
    bbd_im = jnp.einsum('gnp,gh->gphn', bb_im, eye).reshape(G * P, G * N)
    bbd = jnp.concatenate([bbd_re, bbd_im], axis=1).astype(BF16)
    cbd_re = jnp.einsum('gpn,gh->gnhp', c_re, eye).reshape(G * N, G * P)
    cbd_im = jnp.einsum('gpn,gh->gnhp', -c_im, eye).reshape(G * N, G * P)
    cbd = jnp.concatenate([cbd_re, cbd_im], axis=0).astype(BF16)

    def cmul(a, b):
        return a[0] * b[0] - a[1] * b[1], a[0] * b[1] + a[1] * b[0]

    l1 = (lb_re.reshape(1, G * N), lb_im.reshape(1, G * N))
    l2 = cmul(l1, l1)
    l4 = cmul(l2, l2)
    zero = jnp.zeros_like(l1[0])
    lam_tab = jnp.concatenate([l1[0], l1[1], l2[0], l2[1], l4[0], l4[1], zero, zero], axis=0)
    pws = [l1]
    for _ in range(7):
        pws.append(cmul(pws[-1], l1))
    pw_tab = jnp.concatenate([jnp.concatenate([p[0] for p in pws], axis=0),
                              jnp.concatenate([p[1] for p in pws], axis=0)], axis=1)
    return (bbd, cbd, lam_tab, pw_tab, d_skip.reshape(1, W_GROUP), glu_w.astype(BF16),
            glu_b.reshape(1, W_GROUP))


def _gla_kernel(p_ref, wup_ref, bup_ref, tri_ref, sege_ref, segv_ref, ng_ref, o_ref,
                st_ref, b_ref, oacc_ref):
    C = GLA_CHUNK
    nb, tl = p_ref.shape[0], p_ref.shape[1]

    @pl.when(pl.program_id(0) == 0)
    def _():
        st_ref[...] = jnp.zeros_like(st_ref)

    for bi in range(nb):
        x = _dg(p_ref[bi, :, 768:896].astype(BF16), wup_ref[...]) + bup_ref[...]
        log_alpha = (jnp.minimum(x, 0.0) - jnp.log1p(jnp.exp(-jnp.abs(x)))) * (1.0 / GLA_TAU)
        b_ref[bi] = _dot_exact_lhs(tri_ref[...], log_alpha)

    row_i = _iota((C, 128), 0)
    bd_mask = (_iota((256, 128), 0) >> 6) == (_iota((256, 128), 1) >> 5)
    sege = sege_ref[...]

    def chain(bi, r0):
        q = p_ref[bi, pl.ds(r0, C), 0:128] * (GLA_DK ** -0.5)
        k = p_ref[bi, pl.ds(r0, C), 128:256]
        v = p_ref[bi, pl.ds(r0, C), 256:512]
        b = b_ref[bi, pl.ds(r0, C), :]
        parts = []
        for j in range(C):
            e = jnp.exp(b - b[j:j + 1, :])
            parts.append(jnp.where(row_i >= j, (q * k[j:j + 1, :]) * e, 0.0).astype(BF16))
        t = jnp.concatenate(parts, axis=0)
        w = _dg(t, sege)
        st = st_ref[bi]
        qe = (q * jnp.exp(b)).astype(BF16)
        o_inter = _dg(qe, st.astype(BF16), _NT)
        b_last = b[C - 1:C, :]
        kt = (k * jnp.exp(b_last - b)).astype(BF16)
        upd = _dg(v.astype(BF16), kt, _TN)
        yield
        o = o_inter + w[0:C, :] * v[0:1, :]
        for j in range(1, C):
            o = o + w[j * C:(j + 1) * C, :] * v[j:j + 1, :]
        st_ref[bi] = st * jnp.exp(b_last) + jnp.where(bd_mask, upd, 0.0)
        oacc_ref[bi, pl.ds(r0, C), :] = o

    def chunk(c, _):
        r0 = pl.multiple_of(c * C, C)
        _round_robin([chain(bi, r0) for bi in range(nb)])
        return 0

    lax.fori_loop(0, tl // C, chunk, 0)

    for bi in range(nb):
        o = oacc_ref[bi]
        ms = _dot_exact_rhs(o * o, segv_ref[...]) * (1.0 / GLA_DV)
        g = p_ref[bi, :, 512:768]
        o_ref[bi] = o * lax.rsqrt(ms + EPS) * ng_ref[...] * (g * _sigmoid(g))


def _gla_mixer(p, prm, B, L, tl=256):
    wup, bup, ng = prm
    C = GLA_CHUNK
    r = jnp.arange(tl)
    tri = ((r[:, None] // C == r[None, :] // C) & (r[None, :] <= r[:, None])).astype(BF16)
    sege = (jnp.arange(128)[:, None] // GLA_DK == jnp.arange(256)[None, :] // GLA_DV).astype(BF16)
    segv = (jnp.arange(256)[:, None] // GLA_DV == jnp.arange(256)[None, :] // GLA_DV).astype(BF16)
    fixed = lambda l: (0, 0)
    return pl.pallas_call(
        _gla_kernel,
        grid=(L // tl,),
        in_specs=[pl.BlockSpec((B, tl, GLA_COLS_PAD), lambda l: (0, l, 0)),
                  pl.BlockSpec(wup.shape, fixed), pl.BlockSpec(bup.shape, fixed),
                  pl.BlockSpec(tri.shape, fixed), pl.BlockSpec(sege.shape, fixed),
                  pl.BlockSpec(segv.shape, fixed), pl.BlockSpec(ng.shape, fixed)],
        out_specs=pl.BlockSpec((B, tl, W_GROUP), lambda l: (0, l, 0)),
        out_shape=jax.ShapeDtypeStruct((B, L, W_GROUP), F32),
        scratch_shapes=[pltpu.VMEM((B, 256, 128), F32), pltpu.VMEM((B, tl, 128), F32),
                        pltpu.VMEM((B, tl, 256), F32)],
        compiler_params=_cparams("arbitrary"),
        name="gla_mixer",
    )(p.reshape(B, L, GLA_COLS_PAD), wup, bup, tri, sege, segv, ng)


def _gla_params(w_up, b_up, norm_g):
    wup = jnp.zeros((128, 128), F32).at[:GLA_RANK, :].set(w_up).astype(BF16)
    return wup, b_up.reshape(1, 128), norm_g.reshape(1, W_GROUP)


def _rw_chunk(p, st_ref, prev_ref, o_ref, bi, prm):
    (mu, w0, w2, a0, a2, g2, kkg, ka, rk, lng, lnb, seg, tri) = prm
    C = RW_CHUNK
    W = W_GROUP
    prev_row = jnp.broadcast_to(prev_ref[bi, 0:1, :], p.shape)
    p_prev = _shift_rows(p, 1, prev_row)
    prev_ref[bi, 0:1, :] = p[C - 1:C, :]
    xm = p + (p_prev - p) * mu
    r = xm[:, 0:W]
    k = xm[:, W:2 * W]
    v = xm[:, 2 * W:3 * W]
    z = xm[:, 3 * W:3 * W + 128]

    w = -_softplus(-(w0 + _dg(jnp.tanh(z).astype(BF16), w2))) - 0.5
    logd = -jnp.exp(w)
    alr = _sigmoid(a0 + _dg(z.astype(BF16), a2))
    g = _dg(_sigmoid(z).astype(BF16), g2)
    lp = _dot_exact_lhs(tri, logd)
    yield

    kk = k * kkg
    kk = kk / jnp.maximum(jnp.sqrt(_dot1(kk * kk, seg)), 1e-12)
    k2 = k * (1.0 + (alr - 1.0) * ka)
    av = -kk
    bv = kk * alr
    bonus = _dot1(r * k2 * rk, seg) * v
    yield

    lp_c = lp[C - 1:C, :]
    e_neg = jnp.exp(-lp)
    e_rem = jnp.exp(lp_c - lp)
    ah = av * jnp.exp(lp - logd)
    rh = r * jnp.exp(lp)
    bh = bv * e_neg
    kh = k2 * e_neg
    b2 = bv * e_rem
    kc = k2 * e_rem
    p_c = jnp.exp(lp_c)

    lane_head = _iota((C, W), 1) >> 6
    hm = [lane_head == h for h in range(RW_HEADS)]
    zeros = jnp.zeros((C, W), F32)
    lhs = jnp.concatenate([jnp.where(m, ah, zeros) for m in hm] + [jnp.where(m, rh, zeros) for m in hm], axis=0)
    lhs = lhs.astype(BF16)
    xb = _dot1(lhs, bh, _NT).reshape(8, C, C)
    xk = _dot1(lhs, kh, _NT).reshape(8, C, C)
    yield

    ti = _iota((RW_HEADS, C, C), 1)
    tj = _iota((RW_HEADS, C, C), 2)
    strict = ti > tj
    incl = ti >= tj
    lm = jnp.where(strict, xb[0:4], 0.0)
    mak = jnp.where(strict, xk[0:4], 0.0)
    nrb = jnp.where(incl, xb[4:8], 0.0)
    nrk = jnp.where(incl, xk[4:8], 0.0)

    def bmm(a, b):
        return _dot1(a, b, _BNN)

    tinv = jnp.where(ti == tj, 1.0, 0.0) + lm
    lpow = lm
    for _ in range(5):
        lpow = bmm(lpow, lpow)
        yield
        tinv = tinv + bmm(tinv, lpow)
        yield

    def apply(m, x):
        full = _dot1(m.reshape(RW_HEADS * C, C), x).reshape(RW_HEADS, C, W)
        out = jnp.where(hm[0], full[0], zeros)
        for h in range(1, RW_HEADS):
            out = out + jnp.where(hm[h], full[h], zeros)
        return out

    tm = bmm(tinv, mak)
    a2h = apply(tinv, ah)
    nkv = apply(nrk, v)
    yield
    wv = apply(tm, v)
    r2 = rh + apply(nrb, a2h)
    bd = (_iota((W, W), 0) >> 6) == (_iota((W, W), 1) >> 6)
    eye = _iota((W, W), 0) == _iota((W, W), 1)
    a_c = jnp.where(eye, jnp.broadcast_to(p_c, (W, W)), 0.0) + jnp.where(bd, _dot1(b2, a2h, _TN), 0.0)
    yield
    y0 = apply(nrb, wv) + nkv
    g0 = jnp.where(bd, _dot1(b2, wv, _TN) + _dot1(kc, v, _TN), 0.0)
    s0 = st_ref[bi].astype(BF16)
    rs = _dot1(r2, s0)
    st_new = _dot1(a_c, s0)
    yield
    y = rs + y0
    st_ref[bi] = st_new + g0
    mean = _dot1(y, seg) * (1.0 / RW_N)
    yield
    yc = y - mean
    var = _dot1(yc * yc, seg) * (1.0 / RW_N)
    yield
    yn = yc * lax.rsqrt(var + RW_LN_EPS) * lng + lnb
    o_ref[bi] = (yn + bonus) * g


def _rw_kernel(p_ref, mu_ref, w0_ref, w2_ref, a0_ref, a2_ref, g2_ref, kk_ref, ka_ref, rk_ref,
               lng_ref, lnb_ref, seg_ref, tri_ref, o_ref, st_ref, prev_ref):
    @pl.when(pl.program_id(0) == 0)
    def _():
        st_ref[...] = jnp.zeros_like(st_ref)
        prev_ref[...] = jnp.zeros_like(prev_ref)

    prm = tuple(ref[...] for ref in (mu_ref, w0_ref, w2_ref, a0_ref, a2_ref, g2_ref, kk_ref, ka_ref, rk_ref,
                                     lng_ref, lnb_ref, seg_ref, tri_ref))
    _round_robin([_rw_chunk(p_ref[bi], st_ref, prev_ref, o_ref, bi, prm) for bi in range(p_ref.shape[0])])


def _rw_mixer(p, prm, B, L):
    C = RW_CHUNK
    seg = (jnp.arange(256)[:, None] // RW_N == jnp.arange(256)[None, :] // RW_N).astype(BF16)
    tri = (jnp.arange(C)[None, :] <= jnp.arange(C)[:, None]).astype(BF16)
    fixed = lambda l: (0, 0)
    consts = tuple(prm) + (seg, tri)
    return pl.pallas_call(
        _rw_kernel,
        grid=(L // C,),
        in_specs=[pl.BlockSpec((B, C, RW_COLS), lambda l: (0, l, 0))]
        + [pl.BlockSpec(c.shape, fixed) for c in consts],
        out_specs=pl.BlockSpec((B, C, W_GROUP), lambda l: (0, l, 0)),
        out_shape=jax.ShapeDtypeStruct((B, L, W_GROUP), F32),
        scratch_shapes=[pltpu.VMEM((B, W_GROUP, W_GROUP), F32), pltpu.VMEM((B, 8, RW_COLS), F32)],
        compiler_params=_cparams("arbitrary"),
        name="rwkv7_mixer",
    )(p.reshape(B, L, RW_COLS), *consts)


def _rw_params(mu, w0, w2, a0, a2, g2, k_k, k_a, r_k, ln_g, ln_b):
    row = lambda t: t.reshape(1, -1)
    w2p = jnp.zeros((128, W_GROUP), F32).at[0:RW_W_RANK].set(w2).astype(BF16)
    a2p = jnp.zeros((128, W_GROUP), F32).at[RW_W_RANK:RW_W_RANK + RW_A_RANK].set(a2).astype(BF16)
    g2p = jnp.zeros((128, W_GROUP), F32).at[RW_W_RANK + RW_A_RANK:].set(g2).astype(BF16)
    return (row(mu), row(w0), w2p, row(a0), a2p, g2p, row(k_k), row(k_a), row(r_k), row(ln_g), row(ln_b))


def _conv_kernel(p_ref, w_ref, b_ref, lg_ref, lb_ref, o_ref, u_ref, s_ref):
    tl = p_ref.shape[0]
    H = CONV_HALO
    RC = 64
    n = tl + H - SUBLANES

    @pl.when(pl.program_id(1) == 0)
    def _():
        u_ref[0:H, :] = jnp.zeros((H, W_GROUP), F32)

    u_ref[H:H + tl, :] = p_ref[:, 0:W_GROUP] * _sigmoid(p_ref[:, W_GROUP:2 * W_GROUP])
    for b in range(1, SUBLANES):
        for r0 in range(0, n, RC):
            rows = min(RC, n - r0)
            s_ref[b, r0:r0 + rows, :] = u_ref[r0 + b:r0 + b + rows, :]
    off = H - (CONV_WIDTH - 1)
    for c in range(tl // RC):
        acc = jnp.zeros((RC, W_GROUP), F32)
        for j in range(CONV_WIDTH):
            a, b = divmod(off + j, SUBLANES)
            lo = c * RC + a * SUBLANES
            tap = u_ref[lo:lo + RC, :] if b == 0 else s_ref[b, lo:lo + RC, :]
            acc = acc + w_ref[j:j + 1, :] * tap
        y = acc + b_ref[...]
        mu = jnp.mean(y, axis=-1, keepdims=True)
        yc = y - mu
        yn = yc * lax.rsqrt(jnp.mean(yc * yc, axis=-1, keepdims=True) + CONV_LN_EPS) * lg_ref[...] + lb_ref[...]
        o_ref[c * RC:(c + 1) * RC, :] = yn * _sigmoid(yn)
    tail = u_ref[tl:tl + H, :]
    u_ref[0:H, :] = tail


def _conv_mixer(p, prm, B, L, tl=512):
    w, b, lg, lb = prm
    fixed = lambda b_, l: (0, 0)
    return pl.pallas_call(
        _conv_kernel,
        grid=(B, L // tl),
        in_specs=[pl.BlockSpec((None, tl, 2 * W_GROUP), lambda b_, l: (b_, l, 0)),
                  pl.BlockSpec(w.shape, fixed), pl.BlockSpec(b.shape, fixed),
                  pl.BlockSpec(lg.shape, fixed), pl.BlockSpec(lb.shape, fixed)],
        out_specs=pl.BlockSpec((None, tl, W_GROUP), lambda b_, l: (b_, l, 0)),
        out_shape=jax.ShapeDtypeStruct((B, L, W_GROUP), F32),
        scratch_shapes=[pltpu.VMEM((tl + CONV_HALO, W_GROUP), F32),
                        pltpu.VMEM((SUBLANES, tl + CONV_HALO - SUBLANES, W_GROUP), F32)],
        compiler_params=_cparams("parallel", "arbitrary"),
        name="conv_mixer",
    )(p.reshape(B, L, 2 * W_GROUP), w, b, lg, lb)


def _xattn_kernel(h_ref, g_ref, wq_ref, k_ref, v_ref, wo_ref, o_ref):
    x = h_ref[...]
    q = _dg(_rms(x, g_ref[...]).astype(BF16), wq_ref[...])
    acc = x
    for hd in range(X_HEADS):
        lo = hd * X_HEAD_DIM
        s = _dg(q[:, lo:lo + X_HEAD_DIM].astype(BF16), k_ref[:, lo:lo + X_HEAD_DIM], _NT) * (X_HEAD_DIM ** -0.5)
        e = jnp.exp(s - jnp.max(s, axis=-1, keepdims=True))
        pr = e / jnp.sum(e, axis=-1, keepdims=True)
        oh = _dg(pr.astype(BF16), v_ref[:, lo:lo + X_HEAD_DIM])
        acc = acc + _dg(oh.astype(BF16), wo_ref[lo:lo + X_HEAD_DIM, :])
    o_ref[...] = acc


def _xattn(h, g, wq, k, v, wo, B, L, n_mem, tq=512):
    fixed = lambda b, l: (0, 0)
    out = pl.pallas_call(
        _xattn_kernel,
        grid=(B, L // tq),
        in_specs=[pl.BlockSpec((None, tq, D_MODEL), lambda b, l: (b, l, 0)),
                  pl.BlockSpec((1, D_MODEL), fixed), pl.BlockSpec((D_MODEL, D_MODEL), fixed),
                  pl.BlockSpec((None, n_mem, D_MODEL), lambda b, l: (b, 0, 0)),
                  pl.BlockSpec((None, n_mem, D_MODEL), lambda b, l: (b, 0, 0)),
                  pl.BlockSpec((D_MODEL, D_MODEL), fixed)],
        out_specs=pl.BlockSpec((None, tq, D_MODEL), lambda b, l: (b, l, 0)),
        out_shape=jax.ShapeDtypeStruct((B, L, D_MODEL), F32),
        compiler_params=_cparams("parallel", "parallel"),
        name="mem_xattn",
    )(h.reshape(B, L, D_MODEL), g, wq, k.reshape(B, n_mem, D_MODEL), v.reshape(B, n_mem, D_MODEL), wo)
    return out.reshape(B * L, D_MODEL)


def _router_kernel(h_ref, g_ref, wr_ref, br_ref, xn_ref, eid_ref, gate_ref):
    xn = _rms(h_ref[...], g_ref[...])
    xn_ref[...] = xn
    lg = _dot3(wr_ref[...], xn, _NT) + br_ref[...]
    tm = lg.shape[1]
    gl = lg[0:8, :]
    sub = _iota((8, tm), 0)
    gmax = jnp.max(gl, axis=0, keepdims=True)
    gsel = jnp.min(jnp.where(gl == gmax, sub, 8), axis=0, keepdims=True)
    g_w = 1.0 / jnp.sum(jnp.exp(gl - gmax), axis=0, keepdims=True)
    e_in = jnp.zeros((8, tm), F32)
    for grp in range(N_GROUPS):
        e_in = e_in + jnp.where(gsel == grp, lg[8 + 8 * grp:16 + 8 * grp, :], 0.0)
    t1 = jnp.max(e_in, axis=0, keepdims=True)
    i1 = jnp.min(jnp.where(e_in == t1, sub, 8), axis=0, keepdims=True)
    rest = jnp.where(sub == i1, -jnp.inf, e_in)
    t2 = jnp.max(rest, axis=0, keepdims=True)
    i2 = jnp.min(jnp.where(rest == t2, sub, 8), axis=0, keepdims=True)
    e2 = jnp.exp(t2 - t1)
    p1 = 1.0 / (1.0 + e2)
    eid_ref[0:1, :] = gsel * EXP_PER_GROUP + i1
    eid_ref[1:2, :] = gsel * EXP_PER_GROUP + i2
    gate_ref[0:1, :] = p1 * g_w
    gate_ref[1:2, :] = (e2 * p1) * g_w


def _router(h, g, wr, br, tm=ROUTE_TILE):
    T = h.shape[0]
    nt = T // tm
    return pl.pallas_call(
        _router_kernel,
        grid=(nt,),
        in_specs=[pl.BlockSpec((tm, D_MODEL), lambda i: (i, 0)), pl.BlockSpec((1, D_MODEL), lambda i: (0, 0)),
                  pl.BlockSpec(wr.shape, lambda i: (0, 0)), pl.BlockSpec(br.shape, lambda i: (0, 0))],
        out_specs=[pl.BlockSpec((tm, D_MODEL), lambda i: (i, 0)),
                   pl.BlockSpec((None, 2, tm), lambda i: (i, 0, 0)),
                   pl.BlockSpec((None, 2, tm), lambda i: (i, 0, 0))],
        out_shape=[jax.ShapeDtypeStruct((T, D_MODEL), F32),
                   jax.ShapeDtypeStruct((nt, 2, tm), jnp.int32),
                   jax.ShapeDtypeStruct((nt, 2, tm), F32)],
        compiler_params=_cparams("parallel"),
        name="moe_router",
    )(h, g, wr, br)


def _rank_kernel(eid_ref, ut_ref, rank_ref, cnt_ref, carry_ref):
    tm = eid_ref.shape[1]

    @pl.when(pl.program_id(0) == 0)
    def _():
        carry_ref[...] = jnp.zeros_like(carry_ref)

    ex = _iota((N_EXPERTS, tm), 0)
    oh0 = jnp.where(ex == eid_ref[0:1, :], 1.0, 0.0)
    oh1 = jnp.where(ex == eid_ref[1:2, :], 1.0, 0.0)
    oh = oh0 + oh1
    cum = _dg(oh.astype(BF16), ut_ref[...])
    carry = carry_ref[:, 0:1]
    before = cum - oh + carry
    rank_ref[0:1, :] = jnp.sum(oh0 * before, axis=0, keepdims=True).astype(jnp.int32)
    rank_ref[1:2, :] = jnp.sum(oh1 * before, axis=0, keepdims=True).astype(jnp.int32)
    total = carry + cum[:, tm - 1:tm]
    carry_ref[...] = jnp.broadcast_to(total, carry_ref.shape)
    cnt_ref[...] = jnp.broadcast_to(total, cnt_ref.shape).astype(jnp.int32)


def _rank(eid, tm=ROUTE_TILE):
    nt = eid.shape[0]
    ut = (jnp.arange(tm)[:, None] <= jnp.arange(tm)[None, :]).astype(BF16)
    return pl.pallas_call(
        _rank_kernel,
        grid=(nt,),
        in_specs=[pl.BlockSpec((None, 2, tm), lambda i: (i, 0, 0)), pl.BlockSpec((tm, tm), lambda i: (0, 0))],
        out_specs=[pl.BlockSpec((None, 2, tm), lambda i: (i, 0, 0)),
                   pl.BlockSpec((N_EXPERTS, 128), lambda i: (0, 0))],
        out_shape=[jax.ShapeDtypeStruct((nt, 2, tm), jnp.int32),
                   jax.ShapeDtypeStruct((N_EXPERTS, 128), jnp.int32)],
        scratch_shapes=[pltpu.VMEM((N_EXPERTS, 128), F32)],
        compiler_params=_cparams("arbitrary"),
        name="moe_rank",
    )(eid, ut)


def _slot_kernel(ps_ref, eid_ref, rank_ref, slot_ref):
    eid = eid_ref[...]
    acc = rank_ref[...]
    for e in range(N_EXPERTS):
        acc = acc + jnp.where(eid == e, ps_ref[e], 0)
    slot_ref[...] = acc


def _slots(pad_starts, eid, rank):
    nt, _, tm = eid.shape
    shape2d = (nt * 2, tm)
    out = pl.pallas_call(
        _slot_kernel,
        grid_spec=pltpu.PrefetchScalarGridSpec(
            num_scalar_prefetch=1,
            grid=(1,),
            in_specs=[pl.BlockSpec(shape2d, lambda i, ps: (0, 0))] * 2,
            out_specs=pl.BlockSpec(shape2d, lambda i, ps: (0, 0)),
        ),
        out_shape=jax.ShapeDtypeStruct(shape2d, jnp.int32),
        compiler_params=_cparams("arbitrary"),
        name="moe_slots",
    )(pad_starts, eid.reshape(shape2d), rank.reshape(shape2d))
    return out.reshape(nt, 2, tm).transpose(0, 2, 1).reshape(nt * tm * 2)


def _dispatch_kernel(slot_ref, x_ref, xs_in_ref, xs_ref, sem):
    del xs_in_ref
    groups = x_ref.shape[0]

    def issue(i, _):
        for u in range(SUBLANES):
            for k in range(2):
                s = slot_ref[i * (2 * SUBLANES) + (2 * u + k)]
                pltpu.make_async_copy(x_ref.at[i, pl.ds(u, 1)], xs_ref.at[pl.ds(s, 1)], sem).start(priority=k)
        return 0

    lax.fori_loop(0, groups, issue, 0)
    for _ in range(2):
        pltpu.make_async_copy(x_ref, x_ref, sem).wait()


def _dispatch(slot, xn, xs_zero, tm=ROUTE_TILE):
    T = xn.shape[0]
    g = tm // SUBLANES
    return pl.pallas_call(
        _dispatch_kernel,
        grid=(T // tm,),
        in_specs=[pl.BlockSpec((2 * tm,), lambda i: (i,), memory_space=pltpu.SMEM),
                  pl.BlockSpec((g, SUBLANES, D_MODEL), lambda i: (i, 0, 0)),
                  pl.BlockSpec(memory_space=pl.ANY)],
        out_specs=pl.BlockSpec(memory_space=pl.ANY),
        scratch_shapes=[pltpu.SemaphoreType.DMA(())],
        out_shape=jax.ShapeDtypeStruct(xs_zero.shape, F32),
        input_output_aliases={2: 0},
        compiler_params=_cparams("arbitrary"),
        name="moe_dispatch",
    )(slot, xn.reshape(T // SUBLANES, SUBLANES, D_MODEL), xs_zero)


def _expert_kernel(be_ref, nu_ref, x_ref, wg_ref, wu_ref, wd_ref, y_ref, wgb_ref, wub_ref, wdb_ref):
    i = pl.program_id(0)
    used = i < nu_ref[0]

    @pl.when(jnp.logical_and(used, jnp.logical_or(i == 0, be_ref[i] != be_ref[jnp.maximum(i - 1, 0)])))
    def _():
        wgb_ref[...] = wg_ref[...].astype(BF16)
        wub_ref[...] = wu_ref[...].astype(BF16)
        wdb_ref[...] = wd_ref[...].astype(BF16)

    @pl.when(used)
    def _():
        xb = x_ref[...].astype(BF16)
        gate = _dg(xb, wgb_ref[...])
        hid = (gate * _sigmoid(gate)) * _dg(xb, wub_ref[...])
        y_ref[...] = _dg(hid.astype(BF16), wdb_ref[...])

    @pl.when(jnp.logical_not(used))
    def _():
        y_ref[...] = jnp.zeros_like(y_ref)


def _experts(block_exp, n_used, xs, w_gate, w_up, w_down):
    n_slots = xs.shape[0]
    nb = n_slots // MOE_BLOCK
    last = lambda i, nu: jnp.minimum(i, nu[0] - 1)
    return pl.pallas_call(
        _expert_kernel,
        grid_spec=pltpu.PrefetchScalarGridSpec(
            num_scalar_prefetch=2,
            grid=(nb,),
            in_specs=[pl.BlockSpec((MOE_BLOCK, D_MODEL), lambda i, be, nu: (last(i, nu), 0)),
                      pl.BlockSpec((None, D_MODEL, D_EXPERT), lambda i, be, nu: (be[last(i, nu)], 0, 0)),
                      pl.BlockSpec((None, D_MODEL, D_EXPERT), lambda i, be, nu: (be[last(i, nu)], 0, 0)),
                      pl.BlockSpec((None, D_EXPERT, D_MODEL), lambda i, be, nu: (be[last(i, nu)], 0, 0))],
            out_specs=pl.BlockSpec((MOE_BLOCK, D_MODEL), lambda i, be, nu: (i, 0)),
            scratch_shapes=[pltpu.VMEM((D_MODEL, D_EXPERT), BF16), pltpu.VMEM((D_MODEL, D_EXPERT), BF16),
                            pltpu.VMEM((D_EXPERT, D_MODEL), BF16)],
        ),
        out_shape=jax.ShapeDtypeStruct((n_slots, D_MODEL), F32),
        compiler_params=_cparams("arbitrary"),
        name="moe_experts",
    )(block_exp, n_used, xs, w_gate, w_up, w_down)


def _combine_kernel(slot_ref, gate_ref, h_ref, ys_ref, o_ref, y0_ref, y1_ref, sem):
    tm = h_ref.shape[0]
    groups = tm // SUBLANES
    bufs = (y0_ref, y1_ref)

    def issue(i, _):
        for u in range(SUBLANES):
            for k in range(2):
                s = slot_ref[i * (2 * SUBLANES) + (2 * u + k)]
                pltpu.make_async_copy(ys_ref.at[pl.ds(s, 1)], bufs[k].at[i, pl.ds(u, 1)], sem).start(priority=k)
        return 0

    lax.fori_loop(0, groups, issue, 0)
    for k in range(2):
        pltpu.make_async_copy(bufs[k], bufs[k], sem).wait()
    y0 = y0_ref[...].reshape(tm, D_MODEL)
    y1 = y1_ref[...].reshape(tm, D_MODEL)
    o_ref[...] = h_ref[...] + gate_ref[:, 0:1] * y0 + gate_ref[:, 1:2] * y1


def _combine(slot, gate_t, h, ys, tm=ROUTE_TILE):
    T = h.shape[0]
    g = tm // SUBLANES
    return pl.pallas_call(
        _combine_kernel,
        grid=(T // tm,),
        in_specs=[pl.BlockSpec((2 * tm,), lambda i: (i,), memory_space=pltpu.SMEM),
                  pl.BlockSpec((tm, 2), lambda i: (i, 0)),
                  pl.BlockSpec((tm, D_MODEL), lambda i: (i, 0)),
                  pl.BlockSpec(memory_space=pl.ANY)],
        out_specs=pl.BlockSpec((tm, D_MODEL), lambda i: (i, 0)),
        scratch_shapes=[pltpu.VMEM((g, SUBLANES, D_MODEL), F32), pltpu.VMEM((g, SUBLANES, D_MODEL), F32),
                        pltpu.SemaphoreType.DMA(())],
        out_shape=jax.ShapeDtypeStruct((T, D_MODEL), F32),
        compiler_params=_cparams("arbitrary"),
        name="moe_combine",
    )(slot, gate_t, h, ys)


def _moe(h, g, group_w, group_b, expert_w, expert_b, w_gate, w_up, w_down):
    T = h.shape[0]
    wr = jnp.zeros((ROUTE_ROWS, D_MODEL), F32).at[0:N_GROUPS].set(group_w.T).at[8:].set(expert_w.T)
    br = jnp.full((ROUTE_ROWS, 1), -1e30, F32).at[0:N_GROUPS, 0].set(group_b).at[8:, 0].set(expert_b)
    xn, eid, gate = _router(h, g, wr, br)
    rank, cnt = _rank(eid)
    counts = cnt[:, 0]
    padded = (counts + MOE_BLOCK - 1) // MOE_BLOCK * MOE_BLOCK
    pad_ends = jnp.cumsum(padded)
    pad_starts = (pad_ends - padded).astype(jnp.int32)
    n_slots = ((T * 2 + MOE_BLOCK - 1) // MOE_BLOCK + N_EXPERTS) * MOE_BLOCK
    block_start = jnp.arange(n_slots // MOE_BLOCK, dtype=jnp.int32) * MOE_BLOCK
    owner = jnp.sum((pad_ends[None, :] <= block_start[:, None]).astype(jnp.int32), axis=1)
    block_exp = jnp.minimum(owner, N_EXPERTS - 1).astype(jnp.int32)
    slot = _slots(pad_starts, eid, rank)
    xs = _dispatch(slot, xn, jnp.zeros((n_slots, D_MODEL), F32))
    n_used = (pad_ends[N_EXPERTS - 1:] // MOE_BLOCK).astype(jnp.int32)
    ys = _experts(block_exp, n_used, xs, w_gate, w_up, w_down)
    gate_t = gate.transpose(0, 2, 1).reshape(T, 2)
    return _combine(slot, gate_t, h, ys)


def kernel(x, mem, norm_mix_g, w_in, w_out, mix_beta, s5_lam_re, s5_lam_im, s5_b_re, s5_b_im, s5_c_re, s5_c_im, s5_d, s5_log_dt, s5_glu_w, s5_glu_b, gla_w_up, gla_b_up, gla_norm_g, rw_mu, rw_w0, rw_w2, rw_a0, rw_a2, rw_g2, rw_k_k, rw_k_a, rw_r_k, rw_ln_g, rw_ln_b, conv_w, conv_b, conv_ln_g, conv_ln_b, norm_xattn_g, norm_mem_g, xa_wq, xa_wk, xa_wv, xa_wo, norm_ffn_g, moe_group_w, moe_group_b, moe_expert_w, moe_expert_b, moe_w_gate, moe_w_up, moe_w_down, norm_final_g):
    B, L, D = x.shape
    n_mem = mem.shape[1]
    depth = w_in.shape[0]
    T = B * L
    h = x.reshape(T, D)
    mem2d = mem.reshape(B * n_mem, D)
    row = lambda t: t.reshape(1, -1)
    c0 = W_GROUP
    c1 = c0 + 784
    c2 = c1 + RW_COLS
    for l in range(depth):
        wl = w_in[l]
        w5 = wl[:, :c0].astype(BF16)
        wg = jnp.pad(wl[:, c0:c1], ((0, 0), (0, GLA_COLS_PAD - 784))).astype(BF16)
        wr = wl[:, c1:c2].astype(BF16)
        wc = wl[:, c2:].astype(BF16)
        p5, pg, pr, pc = _in_proj(h, row(norm_mix_g[l]), w5, wg, wr, wc)
        y5 = _s5_mixer(p5, _s5_params(s5_lam_re[l], s5_lam_im[l], s5_b_re[l], s5_b_im[l], s5_c_re[l], s5_c_im[l],
                                      s5_d[l], s5_log_dt[l], s5_glu_w[l], s5_glu_b[l]), B, L)
        yg = _gla_mixer(pg, _gla_params(gla_w_up[l], gla_b_up[l], gla_norm_g[l]), B, L)
        yr = _rw_mixer(pr, _rw_params(rw_mu[l], rw_w0[l], rw_w2[l], rw_a0[l], rw_a2[l], rw_g2[l], rw_k_k[l],
                                      rw_k_a[l], rw_r_k[l], rw_ln_g[l], rw_ln_b[l]), B, L)
        yc = _conv_mixer(pc, (conv_w[l], row(conv_b[l]), row(conv_ln_g[l]), row(conv_ln_b[l])), B, L)
        ys = [t.reshape(T, W_GROUP) for t in (y5, yg, yr, yc)]
        h = _out_proj(ys, row(mix_beta[l]), w_out[l].astype(BF16), h)
        kmem, vmem = _kv_proj(mem2d, row(norm_mem_g[l]), xa_wk[l].astype(BF16), xa_wv[l].astype(BF16))
        h = _xattn(h, row(norm_xattn_g[l]), xa_wq[l].astype(BF16), kmem, vmem, xa_wo[l].astype(BF16), B, L, n_mem)
        h = _moe(h, row(norm_ffn_g[l]), moe_group_w[l], moe_group_b[l], moe_expert_w[l], moe_expert_b[l],
                 moe_w_gate[l], moe_w_up[l], moe_w_down[l])
    return _final_norm(h, row(norm_final_g)).reshape(B, L, D)
```

```python
import functools
import math

import jax
import jax.numpy as jnp
from jax import lax
from jax.experimental import pallas as pl
from jax.experimental.pallas import tpu as pltpu

F32 = jnp.float32
BF16 = jnp.bfloat16

D_MODEL = 1024
W_GROUP = 256
EPS = 1e-6

S5_GROUP_CH = 16
S5_GROUPS = 16
S5_STATE = 64
S5_NS = S5_GROUPS * S5_STATE

GLA_HEADS = 4
GLA_DV = 64
GLA_DK = 32
GLA_RANK = 16
GLA_TAU = 16.0
GLA_CHUNK = 16
GLA_COLS_PAD = 896

RW_HEADS = 4
RW_N = 64
RW_W_RANK = 32
RW_A_RANK = 32
RW_G_RANK = 64
RW_LN_EPS = 64e-5
RW_COLS = 896
RW_CHUNK = 64

CONV_WIDTH = 31
CONV_LN_EPS = 1e-5
CONV_HALO = 32

X_HEADS = 4
X_HEAD_DIM = 256

N_GROUPS = 4
EXP_PER_GROUP = 8
N_EXPERTS = 32
D_EXPERT = 512
MOE_BLOCK = 256
ROUTE_ROWS = 40
ROUTE_TILE = 512

SUBLANES = 8
VMEM_LIMIT = 56 * 1024 * 1024


def _cparams(*sem):
    return pltpu.CompilerParams(dimension_semantics=sem, vmem_limit_bytes=VMEM_LIMIT)


_NN = (((1,), (0,)), ((), ()))
_NT = (((1,), (1,)), ((), ()))
_TN = (((0,), (0,)), ((), ()))
_BNN = (((2,), (1,)), ((0,), (0,)))


def _dg(a, b, dims=_NN):
    return lax.dot_general(a, b, dims, preferred_element_type=F32)


def _split2(a):
    hi = a.astype(BF16)
    lo = (a - hi.astype(F32)).astype(BF16)
    return hi, lo


def _split3(a):
    hi = a.astype(BF16)
    r1 = a - hi.astype(F32)
    mid = r1.astype(BF16)
    lo = (r1 - mid.astype(F32)).astype(BF16)
    return hi, mid, lo


def _dot1(a, b, dims=_NN):
    return _dg(a.astype(BF16), b.astype(BF16), dims)


def _dot3(a, b, dims=_NN):
    ah, al = _split2(a)
    bh, bl = _split2(b)
    return _dg(ah, bh, dims) + (_dg(ah, bl, dims) + _dg(al, bh, dims))


def _dot_exact_rhs(a, b_bf16, dims=_NN):
    ah, am, al = _split3(a)
    return _dg(ah, b_bf16, dims) + (_dg(am, b_bf16, dims) + _dg(al, b_bf16, dims))


def _dot_exact_lhs(a_bf16, b, dims=_NN):
    bh, bm, bl = _split3(b)
    return _dg(a_bf16, bh, dims) + (_dg(a_bf16, bm, dims) + _dg(a_bf16, bl, dims))


def _iota(shape, dim):
    return lax.broadcasted_iota(jnp.int32, shape, dim)


def _rms(x, g):
    return x * lax.rsqrt(jnp.mean(x * x, axis=-1, keepdims=True) + EPS) * g


def _sigmoid(x):
    return 1.0 / (1.0 + jnp.exp(-x))


def _softplus(x):
    return jnp.maximum(x, 0.0) + jnp.log1p(jnp.exp(-jnp.abs(x)))


def _round_robin(chains):
    done = object()
    while chains:
        chains = [c for c in chains if next(c, done) is not done]


def _shift_rows(x, s, fill=None):
    rolled = pltpu.roll(x, s, 0)
    rows = _iota(x.shape, 0)
    if fill is None:
        fill = jnp.zeros_like(x)
    return jnp.where(rows >= s, rolled, fill)


def _in_proj_kernel(x_ref, g_ref, w5_ref, wg_ref, wr_ref, wc_ref, o5_ref, og_ref, or_ref, oc_ref):
    xb = _rms(x_ref[...], g_ref[...]).astype(BF16)
    o5_ref[...] = _dg(xb, w5_ref[...])
    og_ref[...] = _dg(xb, wg_ref[...])
    or_ref[...] = _dg(xb, wr_ref[...])
    oc_ref[...] = _dg(xb, wc_ref[...])


def _in_proj(h, g, w5, wg, wr, wc, tm=512):
    T = h.shape[0]
    ws = (w5, wg, wr, wc)
    row = lambda i: (i, 0)
    fixed = lambda i: (0, 0)
    return pl.pallas_call(
        _in_proj_kernel,
        grid=(T // tm,),
        in_specs=[pl.BlockSpec((tm, D_MODEL), row), pl.BlockSpec((1, D_MODEL), fixed)]
        + [pl.BlockSpec(w.shape, fixed) for w in ws],
        out_specs=[pl.BlockSpec((tm, w.shape[1]), row) for w in ws],
        out_shape=[jax.ShapeDtypeStruct((T, w.shape[1]), F32) for w in ws],
        compiler_params=_cparams("parallel"),
        name="in_proj",
    )(h, g, *ws)


def _kv_proj_kernel(m_ref, g_ref, wk_ref, wv_ref, k_ref, v_ref):
    mb = _rms(m_ref[...], g_ref[...]).astype(BF16)
    k_ref[...] = _dg(mb, wk_ref[...]).astype(BF16)
    v_ref[...] = _dg(mb, wv_ref[...]).astype(BF16)


def _kv_proj(mem2d, g, wk, wv, tm=256):
    R = mem2d.shape[0]
    row = lambda i: (i, 0)
    fixed = lambda i: (0, 0)
    return pl.pallas_call(
        _kv_proj_kernel,
        grid=(R // tm,),
        in_specs=[pl.BlockSpec((tm, D_MODEL), row), pl.BlockSpec((1, D_MODEL), fixed),
                  pl.BlockSpec((D_MODEL, D_MODEL), fixed), pl.BlockSpec((D_MODEL, D_MODEL), fixed)],
        out_specs=[pl.BlockSpec((tm, D_MODEL), row)] * 2,
        out_shape=[jax.ShapeDtypeStruct((R, D_MODEL), BF16)] * 2,
        compiler_params=_cparams("parallel"),
        name="kv_proj",
    )(mem2d, g, wk, wv)


def _out_proj_kernel(y5_ref, yg_ref, yr_ref, yc_ref, beta_ref, w_ref, h_ref, o_ref):
    acc = h_ref[...]
    for i, y_ref in enumerate((y5_ref, yg_ref, yr_ref, yc_ref)):
        lo = i * W_GROUP
        yb = (y_ref[...] * beta_ref[:, lo:lo + W_GROUP]).astype(BF16)
        acc = acc + _dg(yb, w_ref[lo:lo + W_GROUP, :])
    o_ref[...] = acc


def _out_proj(ys, beta, w, h, tm=512):
    T = h.shape[0]
    row = lambda i: (i, 0)
    fixed = lambda i: (0, 0)
    return pl.pallas_call(
        _out_proj_kernel,
        grid=(T // tm,),
        in_specs=[pl.BlockSpec((tm, W_GROUP), row)] * 4
        + [pl.BlockSpec((1, D_MODEL), fixed), pl.BlockSpec((D_MODEL, D_MODEL), fixed),
           pl.BlockSpec((tm, D_MODEL), row)],
        out_specs=pl.BlockSpec((tm, D_MODEL), row),
        out_shape=jax.ShapeDtypeStruct((T, D_MODEL), F32),
        compiler_params=_cparams("parallel"),
        name="out_proj",
    )(*ys, beta, w, h)


def _final_norm_kernel(x_ref, g_ref, o_ref):
    o_ref[...] = _rms(x_ref[...], g_ref[...])


def _final_norm(h, g, tm=1024):
    T = h.shape[0]
    return pl.pallas_call(
        _final_norm_kernel,
        grid=(T // tm,),
        in_specs=[pl.BlockSpec((tm, D_MODEL), lambda i: (i, 0)), pl.BlockSpec((1, D_MODEL), lambda i: (0, 0))],
        out_specs=pl.BlockSpec((tm, D_MODEL), lambda i: (i, 0)),
        out_shape=jax.ShapeDtypeStruct((T, D_MODEL), F32),
        compiler_params=_cparams("parallel"),
        name="final_norm",
    )(h, g)


S5_BLOCKS = S5_NS // 128


def _s5_kernel(u_ref, bpad_ref, cpad_ref, lam_ref, d_ref, gw_ref, gb_ref, o_ref, xr_ref, xi_ref, carry_ref):
    nb, tl = u_ref.shape[0], u_ref.shape[1]
    nblk = S5_BLOCKS

    @pl.when(pl.program_id(0) == 0)
    def _():
        carry_ref[...] = jnp.zeros_like(carry_ref)

    for bi in range(nb):
        ub = u_ref[bi].astype(BF16)
        for s in range(nblk):
            bu = _dg(ub, bpad_ref[s])
            xr_ref[bi, pl.ds(s, tl, stride=nblk), :] = bu[:, 0:128]
            xi_ref[bi, pl.ds(s, tl, stride=nblk), :] = bu[:, 128:256]

    lam_re = lam_ref[0]
    lam_im = lam_ref[1]

    def step(t, carry):
        r0 = pl.multiple_of(t * nblk, nblk)
        out = []
        for bi in range(nb):
            c_re, c_im = carry[2 * bi], carry[2 * bi + 1]
            x_re = (lam_re * c_re - lam_im * c_im) + xr_ref[bi, pl.ds(r0, nblk), :]
            x_im = (lam_re * c_im + lam_im * c_re) + xi_ref[bi, pl.ds(r0, nblk), :]
            xr_ref[bi, pl.ds(r0, nblk), :] = x_re
            xi_ref[bi, pl.ds(r0, nblk), :] = x_im
            out += [x_re, x_im]
        return tuple(out)

    init = tuple(carry_ref[i] for i in range(2 * nb))
    last = lax.fori_loop(0, tl, step, init, unroll=8)
    for i in range(2 * nb):
        carry_ref[i] = last[i]

    for bi in range(nb):
        y = d_ref[...] * u_ref[bi]
        for s in range(nblk):
            st = jnp.concatenate([xr_ref[bi, pl.ds(s, tl, stride=nblk), :],
                                  xi_ref[bi, pl.ds(s, tl, stride=nblk), :]], axis=1)
            y = y + _dg(st.astype(BF16), cpad_ref[s])
        y = 0.5 * y * (1.0 + jnp.tanh(math.sqrt(2.0 / math.pi) * (y + 0.044715 * (y * y * y))))
        gate = _dg(y.astype(BF16), gw_ref[...]) + gb_ref[...]
        o_ref[bi] = y * _sigmoid(gate)


def _s5_mixer(p, prm, B, L, tl=256):
    rows = tl * S5_BLOCKS
    return pl.pallas_call(
        _s5_kernel,
        grid=(L // tl,),
        in_specs=[pl.BlockSpec((B, tl, W_GROUP), lambda l: (0, l, 0))]
        + [pl.BlockSpec(c.shape, lambda l, n=c.ndim: (0,) * n) for c in prm],
        out_specs=pl.BlockSpec((B, tl, W_GROUP), lambda l: (0, l, 0)),
        out_shape=jax.ShapeDtypeStruct((B, L, W_GROUP), F32),
        scratch_shapes=[pltpu.VMEM((B, rows, 128), F32), pltpu.VMEM((B, rows, 128), F32),
                        pltpu.VMEM((2 * B, S5_BLOCKS, 128), F32)],
        compiler_params=_cparams("arbitrary"),
        name="s5_mixer",
    )(p.reshape(B, L, W_GROUP), *prm)


def _s5_params(lam_re, lam_im, b_re, b_im, c_re, c_im, d_skip, log_dt, glu_w, glu_b):
    G, N, P = S5_GROUPS, S5_STATE, S5_GROUP_CH
    dt = jnp.exp(log_dt)[:, None]
    mag = jnp.exp(lam_re * dt)
    lb_re = mag * jnp.cos(lam_im * dt)
    lb_im = mag * jnp.sin(lam_im * dt)
    den = lam_re * lam_re + lam_im * lam_im
    n_re, n_im = lb_re - 1.0, lb_im
    f_re = (n_re * lam_re + n_im * lam_im) / den
    f_im = (n_im * lam_re - n_re * lam_im) / den
    bb_re = f_re[:, :, None] * b_re - f_im[:, :, None] * b_im
    bb_im = f_re[:, :, None] * b_im + f_im[:, :, None] * b_re
    eye = jnp.eye(G, dtype=F32)
    bbd_re = jnp.einsum('gnp,gh->gphn', bb_re, eye).reshape(G * P, G * N)
    bbd_im = jnp.einsum('gnp,gh->gphn', bb_im, eye).reshape(G * P, G * N)
    cbd_re = jnp.einsum('gpn,gh->gnhp', c_re, eye).reshape(G * N, G * P)
    cbd_im = jnp.einsum('gpn,gh->gnhp', -c_im, eye).reshape(G * N, G * P)
    nblk = S5_BLOCKS
    bpad = jnp.concatenate([bbd_re.reshape(G * P, nblk, 128).transpose(1, 0, 2),
                            bbd_im.reshape(G * P, nblk, 128).transpose(1, 0, 2)], axis=2).astype(BF16)
    cpad = jnp.concatenate([cbd_re.reshape(nblk, 128, G * P), cbd_im.reshape(nblk, 128, G * P)],
                           axis=1).astype(BF16)
    lam_tab = jnp.stack([lb_re.reshape(nblk, 128), lb_im.reshape(nblk, 128)])
    return (bpad, cpad, lam_tab, d_skip.reshape(1, W_GROUP), glu_w.astype(BF16), glu_b.reshape(1, W_GROUP))


def _gla_kernel(p_ref, wup_ref, bup_ref, tri_ref, sege_ref, segv_ref, ng_ref, o_ref,
                st_ref, b_ref, oacc_ref):
    C = GLA_CHUNK
    nb, tl = p_ref.shape[0], p_ref.shape[1]

    @pl.when(pl.program_id(0) == 0)
    def _():
        st_ref[...] = jnp.zeros_like(st_ref)

    for bi in range(nb):
        x = _dg(p_ref[bi, :, 768:896].astype(BF16), wup_ref[...]) + bup_ref[...]
        log_alpha = (jnp.minimum(x, 0.0) - jnp.log1p(jnp.exp(-jnp.abs(x)))) * (1.0 / GLA_TAU)
        b_ref[bi] = _dot_exact_lhs(tri_ref[...], log_alpha)

    row_i = _iota((C, 128), 0)
    bd_mask = (_iota((256, 128), 0) >> 6) == (_iota((256, 128), 1) >> 5)
    sege = sege_ref[...]

    def chain(bi, r0):
        q = p_ref[bi, pl.ds(r0, C), 0:128] * (GLA_DK ** -0.5)
        k = p_ref[bi, pl.ds(r0, C), 128:256]
        v = p_ref[bi, pl.ds(r0, C), 256:512]
        b = b_ref[bi, pl.ds(r0, C), :]
        parts = []
        for j in range(C):
            e = jnp.exp(b - b[j:j + 1, :])
            parts.append(jnp.where(row_i >= j, (q * k[j:j + 1, :]) * e, 0.0).astype(BF16))
        t = jnp.concatenate(parts, axis=0)
        w = _dg(t, sege)
        st = st_ref[bi]
        qe = (q * jnp.exp(b)).astype(BF16)
        o_inter = _dg(qe, st.astype(BF16), _NT)
        b_last = b[C - 1:C, :]
        kt = (k * jnp.exp(b_last - b)).astype(BF16)
        upd = _dg(v.astype(BF16), kt, _TN)
        yield
        o = o_inter + w[0:C, :] * v[0:1, :]
        for j in range(1, C):
            o = o + w[j * C:(j + 1) * C, :] * v[j:j + 1, :]
        st_ref[bi] = st * jnp.exp(b_last) + jnp.where(bd_mask, upd, 0.0)
        oacc_ref[bi, pl.ds(r0, C), :] = o

    def chunk(c, _):
        r0 = pl.multiple_of(c * C, C)
        _round_robin([chain(bi, r0) for bi in range(nb)])
        return 0

    lax.fori_loop(0, tl // C, chunk, 0)

    for bi in range(nb):
        o = oacc_ref[bi]
        ms = _dot_exact_rhs(o * o, segv_ref[...]) * (1.0 / GLA_DV)
        g = p_ref[bi, :, 512:768]
        o_ref[bi] = o * lax.rsqrt(ms + EPS) * ng_ref[...] * (g * _sigmoid(g))


def _gla_mixer(p, prm, B, L, tl=256):
    wup, bup, ng = prm
    C = GLA_CHUNK
    r = jnp.arange(tl)
    tri = ((r[:, None] // C == r[None, :] // C) & (r[None, :] <= r[:, None])).astype(BF16)
    sege = (jnp.arange(128)[:, None] // GLA_DK == jnp.arange(256)[None, :] // GLA_DV).astype(BF16)
    segv = (jnp.arange(256)[:, None] // GLA_DV == jnp.arange(256)[None, :] // GLA_DV).astype(BF16)
    fixed = lambda l: (0, 0)
    return pl.pallas_call(
        _gla_kernel,
        grid=(L // tl,),
        in_specs=[pl.BlockSpec((B, tl, GLA_COLS_PAD), lambda l: (0, l, 0)),
                  pl.BlockSpec(wup.shape, fixed), pl.BlockSpec(bup.shape, fixed),
                  pl.BlockSpec(tri.shape, fixed), pl.BlockSpec(sege.shape, fixed),
                  pl.BlockSpec(segv.shape, fixed), pl.BlockSpec(ng.shape, fixed)],
        out_specs=pl.BlockSpec((B, tl, W_GROUP), lambda l: (0, l, 0)),
        out_shape=jax.ShapeDtypeStruct((B, L, W_GROUP), F32),
        scratch_shapes=[pltpu.VMEM((B, 256, 128), F32), pltpu.VMEM((B, tl, 128), F32),
                        pltpu.VMEM((B, tl, 256), F32)],
        compiler_params=_cparams("arbitrary"),
        name="gla_mixer",
    )(p.reshape(B, L, GLA_COLS_PAD), wup, bup, tri, sege, segv, ng)


def _gla_params(w_up, b_up, norm_g):
    wup = jnp.zeros((128, 128), F32).at[:GLA_RANK, :].set(w_up).astype(BF16)
    return wup, b_up.reshape(1, 128), norm_g.reshape(1, W_GROUP)


def _rw_chunk(p, st_ref, prev_ref, o_ref, bi, prm):
    (mu, w0, w2, a0, a2, g2, kkg, ka, rk, lng, lnb, seg, tri) = prm
    C = RW_CHUNK
    W = W_GROUP
    prev_row = jnp.broadcast_to(prev_ref[bi, 0:1, :], p.shape)
    p_prev = _shift_rows(p, 1, prev_row)
    prev_ref[bi, 0:1, :] = p[C - 1:C, :]
    xm = p + (p_prev - p) * mu
    r = xm[:, 0:W]
    k = xm[:, W:2 * W]
    v = xm[:, 2 * W:3 * W]
    z = xm[:, 3 * W:3 * W + 128]

    w = -_softplus(-(w0 + _dg(jnp.tanh(z).astype(BF16), w2))) - 0.5
    logd = -jnp.exp(w)
    alr = _sigmoid(a0 + _dg(z.astype(BF16), a2))
    g = _dg(_sigmoid(z).astype(BF16), g2)
    lp = _dot_exact_lhs(tri, logd)
    yield

    kk = k * kkg
    kk = kk / jnp.maximum(jnp.sqrt(_dot1(kk * kk, seg)), 1e-12)
    k2 = k * (1.0 + (alr - 1.0) * ka)
    av = -kk
    bv = kk * alr
    bonus = _dot1(r * k2 * rk, seg) * v
    yield

    lp_c = lp[C - 1:C, :]
    e_neg = jnp.exp(-lp)
    e_rem = jnp.exp(lp_c - lp)
    ah = av * jnp.exp(lp - logd)
    rh = r * jnp.exp(lp)
    bh = bv * e_neg
    kh = k2 * e_neg
    b2 = bv * e_rem
    kc = k2 * e_rem
    p_c = jnp.exp(lp_c)

    lane_head = _iota((C, W), 1) >> 6
    hm = [lane_head == h for h in range(RW_HEADS)]
    zeros = jnp.zeros((C, W), F32)
    lhs = jnp.concatenate([jnp.where(m, ah, zeros) for m in hm] + [jnp.where(m, rh, zeros) for m in hm], axis=0)
    lhs = lhs.astype(BF16)
    xb = _dot1(lhs, bh, _NT).reshape(8, C, C)
    xk = _dot1(lhs, kh, _NT).reshape(8, C, C)
    yield

    ti = _iota((RW_HEADS, C, C), 1)
    tj = _iota((RW_HEADS, C, C), 2)
    strict = ti > tj
    incl = ti >= tj
    lm = jnp.where(strict, xb[0:4], 0.0)
    mak = jnp.where(strict, xk[0:4], 0.0)
    nrb = jnp.where(incl, xb[4:8], 0.0)
    nrk = jnp.where(incl, xk[4:8], 0.0)

    def bmm(a, b):
        return _dot1(a, b, _BNN)

    tinv = jnp.where(ti == tj, 1.0, 0.0) + lm
    lpow = lm
    for _ in range(5):
        lpow = bmm(lpow, lpow)
        yield
        tinv = tinv + bmm(tinv, lpow)
        yield

    def apply(m, x):
        full = _dot1(m.reshape(RW_HEADS * C, C), x).reshape(RW_HEADS, C, W)
        out = jnp.where(hm[0], full[0], zeros)
        for h in range(1, RW_HEADS):
            out = out + jnp.where(hm[h], full[h], zeros)
        return out

    tm = bmm(tinv, mak)
    a2h = apply(tinv, ah)
    nkv = apply(nrk, v)
    yield
    wv = apply(tm, v)
    r2 = rh + apply(nrb, a2h)
    bd = (_iota((W, W), 0) >> 6) == (_iota((W, W), 1) >> 6)
    eye = _iota((W, W), 0) == _iota((W, W), 1)
    a_c = jnp.where(eye, jnp.broadcast_to(p_c, (W, W)), 0.0) + jnp.where(bd, _dot1(b2, a2h, _TN), 0.0)
    yield
    y0 = apply(nrb, wv) + nkv
    g0 = jnp.where(bd, _dot1(b2, wv, _TN) + _dot1(kc, v, _TN), 0.0)
    s0 = st_ref[bi].astype(BF16)
    rs = _dot1(r2, s0)
    st_new = _dot1(a_c, s0)
    yield
    y = rs + y0
    st_ref[bi] = st_new + g0
    mean = _dot1(y, seg) * (1.0 / RW_N)
    yield
    yc = y - mean
    var = _dot1(yc * yc, seg) * (1.0 / RW_N)
    yield
    yn = yc * lax.rsqrt(var + RW_LN_EPS) * lng + lnb
    o_ref[bi] = (yn + bonus) * g


def _rw_kernel(p_ref, mu_ref, w0_ref, w2_ref, a0_ref, a2_ref, g2_ref, kk_ref, ka_ref, rk_ref,
               lng_ref, lnb_ref, seg_ref, tri_ref, o_ref, st_ref, prev_ref):
    @pl.when(pl.program_id(0) == 0)
    def _():
        st_ref[...] = jnp.zeros_like(st_ref)
        prev_ref[...] = jnp.zeros_like(prev_ref)

    prm = tuple(ref[...] for ref in (mu_ref, w0_ref, w2_ref, a0_ref, a2_ref, g2_ref, kk_ref, ka_ref, rk_ref,
                                     lng_ref, lnb_ref, seg_ref, tri_ref))
    _round_robin([_rw_chunk(p_ref[bi], st_ref, prev_ref, o_ref, bi, prm) for bi in range(p_ref.shape[0])])


def _rw_mixer(p, prm, B, L):
    C = RW_CHUNK
    seg = (jnp.arange(256)[:, None] // RW_N == jnp.arange(256)[None, :] // RW_N).astype(BF16)
    tri = (jnp.arange(C)[None, :] <= jnp.arange(C)[:, None]).astype(BF16)
    fixed = lambda l: (0, 0)
    consts = tuple(prm) + (seg, tri)
    return pl.pallas_call(
        _rw_kernel,
        grid=(L // C,),
        in_specs=[pl.BlockSpec((B, C, RW_COLS), lambda l: (0, l, 0))]
        + [pl.BlockSpec(c.shape, fixed) for c in consts],
        out_specs=pl.BlockSpec((B, C, W_GROUP), lambda l: (0, l, 0)),
        out_shape=jax.ShapeDtypeStruct((B, L, W_GROUP), F32),
        scratch_shapes=[pltpu.VMEM((B, W_GROUP, W_GROUP), F32), pltpu.VMEM((B, 8, RW_COLS), F32)],
        compiler_params=_cparams("arbitrary"),
        name="rwkv7_mixer",
    )(p.reshape(B, L, RW_COLS), *consts)


def _rw_params(mu, w0, w2, a0, a2, g2, k_k, k_a, r_k, ln_g, ln_b):
    row = lambda t: t.reshape(1, -1)
    w2p = jnp.zeros((128, W_GROUP), F32).at[0:RW_W_RANK].set(w2).astype(BF16)
    a2p = jnp.zeros((128, W_GROUP), F32).at[RW_W_RANK:RW_W_RANK + RW_A_RANK].set(a2).astype(BF16)
    g2p = jnp.zeros((128, W_GROUP), F32).at[RW_W_RANK + RW_A_RANK:].set(g2).astype(BF16)
    return (row(mu), row(w0), w2p, row(a0), a2p, g2p, row(k_k), row(k_a), row(r_k), row(ln_g), row(ln_b))


def _conv_kernel(p_ref, w_ref, b_ref, lg_ref, lb_ref, o_ref, u_ref, s_ref):
    tl = p_ref.shape[0]
    H = CONV_HALO
    RC = 64
    n = tl + H - SUBLANES

    @pl.when(pl.program_id(1) == 0)
    def _():
        u_ref[0:H, :] = jnp.zeros((H, W_GROUP), F32)

    u_ref[H:H + tl, :] = p_ref[:, 0:W_GROUP] * _sigmoid(p_ref[:, W_GROUP:2 * W_GROUP])
    for b in range(1, SUBLANES):
        for r0 in range(0, n, RC):
            rows = min(RC, n - r0)
            s_ref[b, r0:r0 + rows, :] = u_ref[r0 + b:r0 + b + rows, :]
    off = H - (CONV_WIDTH - 1)
    for c in range(tl // RC):
        acc = jnp.zeros((RC, W_GROUP), F32)
        for j in range(CONV_WIDTH):
            a, b = divmod(off + j, SUBLANES)
            lo = c * RC + a * SUBLANES
            tap = u_ref[lo:lo + RC, :] if b == 0 else s_ref[b, lo:lo + RC, :]
            acc = acc + w_ref[j:j + 1, :] * tap
        y = acc + b_ref[...]
        mu = jnp.mean(y, axis=-1, keepdims=True)
        yc = y - mu
        yn = yc * lax.rsqrt(jnp.mean(yc * yc, axis=-1, keepdims=True) + CONV_LN_EPS) * lg_ref[...] + lb_ref[...]
        o_ref[c * RC:(c + 1) * RC, :] = yn * _sigmoid(yn)
    tail = u_ref[tl:tl + H, :]
    u_ref[0:H, :] = tail


def _conv_mixer(p, prm, B, L, tl=512):
    w, b, lg, lb = prm
    fixed = lambda b_, l: (0, 0)
    return pl.pallas_call(
        _conv_kernel,
        grid=(B, L // tl),
        in_specs=[pl.BlockSpec((None, tl, 2 * W_GROUP), lambda b_, l: (b_, l, 0)),
                  pl.BlockSpec(w.shape, fixed), pl.BlockSpec(b.shape, fixed),
                  pl.BlockSpec(lg.shape, fixed), pl.BlockSpec(lb.shape, fixed)],
        out_specs=pl.BlockSpec((None, tl, W_GROUP), lambda b_, l: (b_, l, 0)),
        out_shape=jax.ShapeDtypeStruct((B, L, W_GROUP), F32),
        scratch_shapes=[pltpu.VMEM((tl + CONV_HALO, W_GROUP), F32),
                        pltpu.VMEM((SUBLANES, tl + CONV_HALO - SUBLANES, W_GROUP), F32)],
        compiler_params=_cparams("parallel", "arbitrary"),
        name="conv_mixer",
    )(p.reshape(B, L, 2 * W_GROUP), w, b, lg, lb)


def _xattn_kernel(h_ref, g_ref, wq_ref, k_ref, v_ref, wo_ref, o_ref):
    x = h_ref[...]
    q = _dg(_rms(x, g_ref[...]).astype(BF16), wq_ref[...])
    acc = x
    for hd in range(X_HEADS):
        lo = hd * X_HEAD_DIM
        s = _dg(q[:, lo:lo + X_HEAD_DIM].astype(BF16), k_ref[:, lo:lo + X_HEAD_DIM], _NT) * (X_HEAD_DIM ** -0.5)
        e = jnp.exp(s - jnp.max(s, axis=-1, keepdims=True))
        pr = e / jnp.sum(e, axis=-1, keepdims=True)
        oh = _dg(pr.astype(BF16), v_ref[:, lo:lo + X_HEAD_DIM])
        acc = acc + _dg(oh.astype(BF16), wo_ref[lo:lo + X_HEAD_DIM, :])
    o_ref[...] = acc


def _xattn(h, g, wq, k, v, wo, B, L, n_mem, tq=512):
    fixed = lambda b, l: (0, 0)
    out = pl.pallas_call(
        _xattn_kernel,
        grid=(B, L // tq),
        in_specs=[pl.BlockSpec((None, tq, D_MODEL), lambda b, l: (b, l, 0)),
                  pl.BlockSpec((1, D_MODEL), fixed), pl.BlockSpec((D_MODEL, D_MODEL), fixed),
                  pl.BlockSpec((None, n_mem, D_MODEL), lambda b, l: (b, 0, 0)),
                  pl.BlockSpec((None, n_mem, D_MODEL), lambda b, l: (b, 0, 0)),
                  pl.BlockSpec((D_MODEL, D_MODEL), fixed)],
        out_specs=pl.BlockSpec((None, tq, D_MODEL), lambda b, l: (b, l, 0)),
        out_shape=jax.ShapeDtypeStruct((B, L, D_MODEL), F32),
        compiler_params=_cparams("parallel", "parallel"),
        name="mem_xattn",
    )(h.reshape(B, L, D_MODEL), g, wq, k.reshape(B, n_mem, D_MODEL), v.reshape(B, n_mem, D_MODEL), wo)
    return out.reshape(B * L, D_MODEL)


def _router_kernel(h_ref, g_ref, wr_ref, br_ref, xn_ref, eid_ref, gate_ref):
    xn = _rms(h_ref[...], g_ref[...])
    xn_ref[...] = xn
    lg = _dot1(wr_ref[...], xn, _NT) + br_ref[...]
    tm = lg.shape[1]
    gl = lg[0:8, :]
    sub = _iota((8, tm), 0)
    gmax = jnp.max(gl, axis=0, keepdims=True)
    gsel = jnp.min(jnp.where(gl == gmax, sub, 8), axis=0, keepdims=True)
    g_w = 1.0 / jnp.sum(jnp.exp(gl - gmax), axis=0, keepdims=True)
    e_in = jnp.zeros((8, tm), F32)
    for grp in range(N_GROUPS):
        e_in = e_in + jnp.where(gsel == grp, lg[8 + 8 * grp:16 + 8 * grp, :], 0.0)
    t1 = jnp.max(e_in, axis=0, keepdims=True)
    i1 = jnp.min(jnp.where(e_in == t1, sub, 8), axis=0, keepdims=True)
    rest = jnp.where(sub == i1, -jnp.inf, e_in)
    t2 = jnp.max(rest, axis=0, keepdims=True)
    i2 = jnp.min(jnp.where(rest == t2, sub, 8), axis=0, keepdims=True)
    e2 = jnp.exp(t2 - t1)
    p1 = 1.0 / (1.0 + e2)
    eid_ref[0:1, :] = gsel * EXP_PER_GROUP + i1
    eid_ref[1:2, :] = gsel * EXP_PER_GROUP + i2
    gate_ref[0:1, :] = p1 * g_w
    gate_ref[1:2, :] = (e2 * p1) * g_w


def _router(h, g, wr, br, tm=ROUTE_TILE):
    T = h.shape[0]
    nt = T // tm
    return pl.pallas_call(
        _router_kernel,
        grid=(nt,),
        in_specs=[pl.BlockSpec((tm, D_MODEL), lambda i: (i, 0)), pl.BlockSpec((1, D_MODEL), lambda i: (0, 0)),
                  pl.BlockSpec(wr.shape, lambda i: (0, 0)), pl.BlockSpec(br.shape, lambda i: (0, 0))],
        out_specs=[pl.BlockSpec((tm, D_MODEL), lambda i: (i, 0)),
                   pl.BlockSpec((None, 2, tm), lambda i: (i, 0, 0)),
                   pl.BlockSpec((None, 2, tm), lambda i: (i, 0, 0))],
        out_shape=[jax.ShapeDtypeStruct((T, D_MODEL), F32),
                   jax.ShapeDtypeStruct((nt, 2, tm), jnp.int32),
                   jax.ShapeDtypeStruct((nt, 2, tm), F32)],
        compiler_params=_cparams("parallel"),
        name="moe_router",
    )(h, g, wr, br)


def _rank_kernel(eid_ref, ut_ref, rank_ref, cnt_ref, carry_ref):
    tm = eid_ref.shape[1]

    @pl.when(pl.program_id(0) == 0)
    def _():
        carry_ref[...] = jnp.zeros_like(carry_ref)

    ex = _iota((N_EXPERTS, tm), 0)
    oh0 = jnp.where(ex == eid_ref[0:1, :], 1.0, 0.0)
    oh1 = jnp.where(ex == eid_ref[1:2, :], 1.0, 0.0)
    oh = oh0 + oh1
    cum = _dg(oh.astype(BF16), ut_ref[...])
    carry = carry_ref[:, 0:1]
    before = cum - oh + carry
    rank_ref[0:1, :] = jnp.sum(oh0 * before, axis=0, keepdims=True).astype(jnp.int32)
    rank_ref[1:2, :] = jnp.sum(oh1 * before, axis=0, keepdims=True).astype(jnp.int32)
    total = carry + cum[:, tm - 1:tm]
    carry_ref[...] = jnp.broadcast_to(total, carry_ref.shape)
    cnt_ref[...] = jnp.broadcast_to(total, cnt_ref.shape).astype(jnp.int32)


def _rank(eid, tm=ROUTE_TILE):
    nt = eid.shape[0]
    ut = (jnp.arange(tm)[:, None] <= jnp.arange(tm)[None, :]).astype(BF16)
    return pl.pallas_call(
        _rank_kernel,
        grid=(nt,),
        in_specs=[pl.BlockSpec((None, 2, tm), lambda i: (i, 0, 0)), pl.BlockSpec((tm, tm), lambda i: (0, 0))],
        out_specs=[pl.BlockSpec((None, 2, tm), lambda i: (i, 0, 0)),
                   pl.BlockSpec((N_EXPERTS, 128), lambda i: (0, 0))],
        out_shape=[jax.ShapeDtypeStruct((nt, 2, tm), jnp.int32),
                   jax.ShapeDtypeStruct((N_EXPERTS, 128), jnp.int32)],
        scratch_shapes=[pltpu.VMEM((N_EXPERTS, 128), F32)],
        compiler_params=_cparams("arbitrary"),
        name="moe_rank",
    )(eid, ut)


def _slot_kernel(ps_ref, eid_ref, rank_ref, slot_ref):
    eid = eid_ref[...]
    acc = rank_ref[...]
    for e in range(N_EXPERTS):
        acc = acc + jnp.where(eid == e, ps_ref[e], 0)
    slot_ref[...] = acc


def _slots(pad_starts, eid, rank):
    nt, _, tm = eid.shape
    shape2d = (nt * 2, tm)
    out = pl.pallas_call(
        _slot_kernel,
        grid_spec=pltpu.PrefetchScalarGridSpec(
            num_scalar_prefetch=1,
            grid=(1,),
            in_specs=[pl.BlockSpec(shape2d, lambda i, ps: (0, 0))] * 2,
            out_specs=pl.BlockSpec(shape2d, lambda i, ps: (0, 0)),
        ),
        out_shape=jax.ShapeDtypeStruct(shape2d, jnp.int32),
        compiler_params=_cparams("arbitrary"),
        name="moe_slots",
    )(pad_starts, eid.reshape(shape2d), rank.reshape(shape2d))
    return out.reshape(nt, 2, tm).transpose(0, 2, 1).reshape(nt * tm * 2)


def _dispatch_kernel(lo_ref, np_ref, nu_ref, slot_ref, x_ref, xs_ref, z_ref, sem):
    groups = x_ref.shape[0]

    @pl.when(pl.program_id(0) == 0)
    def _():
        z_ref[...] = jnp.zeros_like(z_ref)

        def pad_copy(e, r):
            return pltpu.make_async_copy(z_ref.at[pl.ds(0, 1)], xs_ref.at[pl.ds(lo_ref[e] + r, 1)], sem)

        def start_row(e):
            def body(r, c):
                pad_copy(e, r).start()
                return c
            return body

        def wait_row(e):
            def body(r, c):
                pad_copy(e, r).wait()
                return c
            return body

        def per_expert(e, c):
            lax.fori_loop(0, np_ref[e], start_row(e), 0)
            lax.fori_loop(0, np_ref[e], wait_row(e), 0)
            return c

        lax.fori_loop(0, N_EXPERTS, per_expert, 0)

        def tail_copy(blk):
            return pltpu.make_async_copy(z_ref, xs_ref.at[pl.ds(pl.multiple_of(blk * MOE_BLOCK, MOE_BLOCK), MOE_BLOCK)], sem)

        def tail_block(blk, c):
            tail_copy(blk).start()
            tail_copy(blk).wait()
            return c

        lax.fori_loop(nu_ref[0], xs_ref.shape[0] // MOE_BLOCK, tail_block, 0)

    def issue(i, _):
        for u in range(SUBLANES):
            for k in range(2):
                s = slot_ref[i * (2 * SUBLANES) + (2 * u + k)]
                pltpu.make_async_copy(x_ref.at[i, pl.ds(u, 1)], xs_ref.at[pl.ds(s, 1)], sem).start(priority=k)
        return 0

    lax.fori_loop(0, groups, issue, 0)
    for _ in range(2):
        pltpu.make_async_copy(x_ref, x_ref, sem).wait()


def _dispatch(pad_lo, n_pad, n_used, slot, xn, n_slots, tm=ROUTE_TILE):
    T = xn.shape[0]
    g = tm // SUBLANES
    return pl.pallas_call(
        _dispatch_kernel,
        grid_spec=pltpu.PrefetchScalarGridSpec(
            num_scalar_prefetch=3,
            grid=(T // tm,),
            in_specs=[pl.BlockSpec((2 * tm,), lambda i, lo, npad, nu: (i,), memory_space=pltpu.SMEM),
                      pl.BlockSpec((g, SUBLANES, D_MODEL), lambda i, lo, npad, nu: (i, 0, 0))],
            out_specs=pl.BlockSpec(memory_space=pl.ANY),
            scratch_shapes=[pltpu.VMEM((MOE_BLOCK, D_MODEL), F32), pltpu.SemaphoreType.DMA(())],
        ),
        out_shape=jax.ShapeDtypeStruct((n_slots, D_MODEL), F32),
        compiler_params=_cparams("arbitrary"),
        name="moe_dispatch",
    )(pad_lo, n_pad, n_used, slot, xn.reshape(T // SUBLANES, SUBLANES, D_MODEL))


def _expert_kernel(be_ref, nu_ref, x_ref, wg_ref, wu_ref, wd_ref, y_ref, wgb_ref, wub_ref, wdb_ref):
    i = pl.program_id(0)
    used = i < nu_ref[0]

    @pl.when(jnp.logical_and(used, jnp.logical_or(i == 0, be_ref[i] != be_ref[jnp.maximum(i - 1, 0)])))
    def _():
        wgb_ref[...] = wg_ref[...].astype(BF16)
        wub_ref[...] = wu_ref[...].astype(BF16)
        wdb_ref[...] = wd_ref[...].astype(BF16)

    @pl.when(used)
    def _():
        xb = x_ref[...].astype(BF16)
        gate = _dg(xb, wgb_ref[...])
        hid = (gate * _sigmoid(gate)) * _dg(xb, wub_ref[...])
        y_ref[...] = _dg(hid.astype(BF16), wdb_ref[...])

    @pl.when(jnp.logical_not(used))
    def _():
        y_ref[...] = jnp.zeros_like(y_ref)


def _experts(block_exp, n_used, xs, w_gate, w_up, w_down, layer):
    n_slots = xs.shape[0]
    nb = n_slots // MOE_BLOCK
    last = lambda i, nu: jnp.minimum(i, nu[0] - 1)
    wmap = lambda i, be, nu: (layer, be[last(i, nu)], 0, 0)
    return pl.pallas_call(
        _expert_kernel,
        grid_spec=pltpu.PrefetchScalarGridSpec(
            num_scalar_prefetch=2,
            grid=(nb,),
            in_specs=[pl.BlockSpec((MOE_BLOCK, D_MODEL), lambda i, be, nu: (last(i, nu), 0)),
                      pl.BlockSpec((None, None, D_MODEL, D_EXPERT), wmap),
                      pl.BlockSpec((None, None, D_MODEL, D_EXPERT), wmap),
                      pl.BlockSpec((None, None, D_EXPERT, D_MODEL), wmap)],
            out_specs=pl.BlockSpec((MOE_BLOCK, D_MODEL), lambda i, be, nu: (i, 0)),
            scratch_shapes=[pltpu.VMEM((D_MODEL, D_EXPERT), BF16), pltpu.VMEM((D_MODEL, D_EXPERT), BF16),
                            pltpu.VMEM((D_EXPERT, D_MODEL), BF16)],
        ),
        out_shape=jax.ShapeDtypeStruct((n_slots, D_MODEL), F32),
        compiler_params=_cparams("arbitrary"),
        name="moe_experts",
    )(block_exp, n_used, xs, w_gate, w_up, w_down)


def _combine_kernel(slot_ref, gate_ref, h_ref, ys_ref, o_ref, y0_ref, y1_ref, sem):
    tm = h_ref.shape[0]
    groups = tm // SUBLANES
    bufs = (y0_ref, y1_ref)

    def issue(i, _):
        for u in range(SUBLANES):
            for k in range(2):
                s = slot_ref[i * (2 * SUBLANES) + (2 * u + k)]
                pltpu.make_async_copy(ys_ref.at[pl.ds(s, 1)], bufs[k].at[i, pl.ds(u, 1)], sem).start(priority=k)
        return 0

    lax.fori_loop(0, groups, issue, 0)
    for k in range(2):
        pltpu.make_async_copy(bufs[k], bufs[k], sem).wait()
    y0 = y0_ref[...].reshape(tm, D_MODEL)
    y1 = y1_ref[...].reshape(tm, D_MODEL)
    o_ref[...] = h_ref[...] + gate_ref[:, 0:1] * y0 + gate_ref[:, 1:2] * y1


def _combine(slot, gate_t, h, ys, tm=ROUTE_TILE):
    T = h.shape[0]
    g = tm // SUBLANES
    return pl.pallas_call(
        _combine_kernel,
        grid=(T // tm,),
        in_specs=[pl.BlockSpec((2 * tm,), lambda i: (i,), memory_space=pltpu.SMEM),
                  pl.BlockSpec((tm, 2), lambda i: (i, 0)),
                  pl.BlockSpec((tm, D_MODEL), lambda i: (i, 0)),
                  pl.BlockSpec(memory_space=pl.ANY)],
        out_specs=pl.BlockSpec((tm, D_MODEL), lambda i: (i, 0)),
        scratch_shapes=[pltpu.VMEM((g, SUBLANES, D_MODEL), F32), pltpu.VMEM((g, SUBLANES, D_MODEL), F32),
                        pltpu.SemaphoreType.DMA(())],
        out_shape=jax.ShapeDtypeStruct((T, D_MODEL), F32),
        compiler_params=_cparams("arbitrary"),
        name="moe_combine",
    )(slot, gate_t, h, ys)


def _moe(h, g, group_w, group_b, expert_w, expert_b, w_gate, w_up, w_down, layer):
    T = h.shape[0]
    wr = jnp.zeros((ROUTE_ROWS, D_MODEL), F32).at[0:N_GROUPS].set(group_w.T).at[8:].set(expert_w.T)
    br = jnp.full((ROUTE_ROWS, 1), -1e30, F32).at[0:N_GROUPS, 0].set(group_b).at[8:, 0].set(expert_b)
    xn, eid, gate = _router(h, g, wr, br)
    rank, cnt = _rank(eid)
    counts = cnt[:, 0]
    padded = (counts + MOE_BLOCK - 1) // MOE_BLOCK * MOE_BLOCK
    pad_ends = jnp.cumsum(padded)
    pad_starts = (pad_ends - padded).astype(jnp.int32)
    n_slots = ((T * 2 + MOE_BLOCK - 1) // MOE_BLOCK + N_EXPERTS) * MOE_BLOCK
    block_start = jnp.arange(n_slots // MOE_BLOCK, dtype=jnp.int32) * MOE_BLOCK
    owner = jnp.sum((pad_ends[None, :] <= block_start[:, None]).astype(jnp.int32), axis=1)
    block_exp = jnp.minimum(owner, N_EXPERTS - 1).astype(jnp.int32)
    slot = _slots(pad_starts, eid, rank)
    n_used = (pad_ends[N_EXPERTS - 1:] // MOE_BLOCK).astype(jnp.int32)
    xs = _dispatch((pad_starts + counts).astype(jnp.int32), (padded - counts).astype(jnp.int32), n_used,
                   slot, xn, n_slots)
    ys = _experts(block_exp, n_used, xs, w_gate, w_up, w_down, layer)
    gate_t = gate.transpose(0, 2, 1).reshape(T, 2)
    return _combine(slot, gate_t, h, ys)


def kernel(x, mem, norm_mix_g, w_in, w_out, mix_beta, s5_lam_re, s5_lam_im, s5_b_re, s5_b_im, s5_c_re, s5_c_im, s5_d, s5_log_dt, s5_glu_w, s5_glu_b, gla_w_up, gla_b_up, gla_norm_g, rw_mu, rw_w0, rw_w2, rw_a0, rw_a2, rw_g2, rw_k_k, rw_k_a, rw_r_k, rw_ln_g, rw_ln_b, conv_w, conv_b, conv_ln_g, conv_ln_b, norm_xattn_g, norm_mem_g, xa_wq, xa_wk, xa_wv, xa_wo, norm_ffn_g, moe_group_w, moe_group_b, moe_expert_w, moe_expert_b, moe_w_gate, moe_w_up, moe_w_down, norm_final_g):
    B, L, D = x.shape
    n_mem = mem.shape[1]
    depth = w_in.shape[0]
    T = B * L
    h = x.reshape(T, D)
    mem2d = mem.reshape(B * n_mem, D)
    row = lambda t: t.reshape(1, -1)
    c0 = W_GROUP
    c1 = c0 + 784
    c2 = c1 + RW_COLS
    for l in range(depth):
        wl = w_in[l]
        w5 = wl[:, :c0].astype(BF16)
        wg = jnp.pad(wl[:, c0:c1], ((0, 0), (0, GLA_COLS_PAD - 784))).astype(BF16)
        wr = wl[:, c1:c2].astype(BF16)
        wc = wl[:, c2:].astype(BF16)
        p5, pg, pr, pc = _in_proj(h, row(norm_mix_g[l]), w5, wg, wr, wc)
        y5 = _s5_mixer(p5, _s5_params(s5_lam_re[l], s5_lam_im[l], s5_b_re[l], s5_b_im[l], s5_c_re[l], s5_c_im[l],
                                      s5_d[l], s5_log_dt[l], s5_glu_w[l], s5_glu_b[l]), B, L)
        yg = _gla_mixer(pg, _gla_params(gla_w_up[l], gla_b_up[l], gla_norm_g[l]), B, L)
        yr = _rw_mixer(pr, _rw_params(rw_mu[l], rw_w0[l], rw_w2[l], rw_a0[l], rw_a2[l], rw_g2[l], rw_k_k[l],
                                      rw_k_a[l], rw_r_k[l], rw_ln_g[l], rw_ln_b[l]), B, L)
        yc = _conv_mixer(pc, (conv_w[l], row(conv_b[l]), row(conv_ln_g[l]), row(conv_ln_b[l])), B, L)
        ys = [t.reshape(T, W_GROUP) for t in (y5, yg, yr, yc)]
        h = _out_proj(ys, row(mix_beta[l]), w_out[l].astype(BF16), h)
        kmem, vmem = _kv_proj(mem2d, row(norm_mem_g[l]), xa_wk[l].astype(BF16), xa_wv[l].astype(BF16))
        h = _xattn(h, row(norm_xattn_g[l]), xa_wq[l].astype(BF16), kmem, vmem, xa_wo[l].astype(BF16), B, L, n_mem)
        h = _moe(h, row(norm_ffn_g[l]), moe_group_w[l], moe_group_b[l], moe_expert_w[l], moe_expert_b[l],
                 moe_w_gate, moe_w_up, moe_w_down, l)
    return _final_norm(h, row(norm_final_g)).reshape(B, L, D)
```

```python
import functools
import math

import jax
import jax.numpy as jnp
from jax import lax
from jax.experimental import pallas as pl
from jax.experimental.pallas import tpu as pltpu

F32 = jnp.float32
BF16 = jnp.bfloat16
ACT = jnp.bfloat16

D_MODEL = 1024
W_GROUP = 256
EPS = 1e-6

S5_GROUP_CH = 16
S5_GROUPS = 16
S5_STATE = 64
S5_NS = S5_GROUPS * S5_STATE

GLA_HEADS = 4
GLA_DV = 64
GLA_DK = 32
GLA_RANK = 16
GLA_TAU = 16.0
GLA_CHUNK = 16
GLA_COLS_PAD = 896

RW_HEADS = 4
RW_N = 64
RW_W_RANK = 32
RW_A_RANK = 32
RW_G_RANK = 64
RW_LN_EPS = 64e-5
RW_COLS = 896
RW_CHUNK = 64

CONV_WIDTH = 31
CONV_LN_EPS = 1e-5
CONV_HALO = 32

X_HEADS = 4
X_HEAD_DIM = 256

N_GROUPS = 4
EXP_PER_GROUP = 8
N_EXPERTS = 32
D_EXPERT = 512
MOE_BLOCK = 256
ROUTE_ROWS = 40
ROUTE_TILE = 512

SUBLANES = 8
VMEM_LIMIT = 56 * 1024 * 1024


def _cparams(*sem):
    return pltpu.CompilerParams(dimension_semantics=sem, vmem_limit_bytes=VMEM_LIMIT)


_NN = (((1,), (0,)), ((), ()))
_NT = (((1,), (1,)), ((), ()))
_TN = (((0,), (0,)), ((), ()))
_BNN = (((2,), (1,)), ((0,), (0,)))


def _dg(a, b, dims=_NN):
    return lax.dot_general(a, b, dims, preferred_element_type=F32)


def _split2(a):
    hi = a.astype(BF16)
    lo = (a - hi.astype(F32)).astype(BF16)
    return hi, lo


def _split3(a):
    hi = a.astype(BF16)
    r1 = a - hi.astype(F32)
    mid = r1.astype(BF16)
    lo = (r1 - mid.astype(F32)).astype(BF16)
    return hi, mid, lo


def _dot1(a, b, dims=_NN):
    return _dg(a.astype(BF16), b.astype(BF16), dims)


def _dot3(a, b, dims=_NN):
    ah, al = _split2(a)
    bh, bl = _split2(b)
    return _dg(ah, bh, dims) + (_dg(ah, bl, dims) + _dg(al, bh, dims))


def _dot_exact_rhs(a, b_bf16, dims=_NN):
    ah, am, al = _split3(a)
    return _dg(ah, b_bf16, dims) + (_dg(am, b_bf16, dims) + _dg(al, b_bf16, dims))


def _dot_exact_lhs(a_bf16, b, dims=_NN):
    bh, bm, bl = _split3(b)
    return _dg(a_bf16, bh, dims) + (_dg(a_bf16, bm, dims) + _dg(a_bf16, bl, dims))


def _iota(shape, dim):
    return lax.broadcasted_iota(jnp.int32, shape, dim)


def _rms(x, g):
    return x * lax.rsqrt(jnp.mean(x * x, axis=-1, keepdims=True) + EPS) * g


def _sigmoid(x):
    return 1.0 / (1.0 + jnp.exp(-x))


def _softplus(x):
    return jnp.maximum(x, 0.0) + jnp.log1p(jnp.exp(-jnp.abs(x)))


def _round_robin(chains):
    done = object()
    while chains:
        chains = [c for c in chains if next(c, done) is not done]


def _shift_rows(x, s, fill=None):
    rolled = pltpu.roll(x, s, 0)
    rows = _iota(x.shape, 0)
    if fill is None:
        fill = jnp.zeros_like(x)
    return jnp.where(rows >= s, rolled, fill)


def _in_proj_kernel(x_ref, g_ref, w5_ref, wg_ref, wr_ref, wc_ref, o5_ref, og_ref, or_ref, oc_ref):
    xb = _rms(x_ref[...], g_ref[...]).astype(BF16)
    o5_ref[...] = _dg(xb, w5_ref[...]).astype(o5_ref.dtype)
    og_ref[...] = _dg(xb, wg_ref[...]).astype(og_ref.dtype)
    or_ref[...] = _dg(xb, wr_ref[...]).astype(or_ref.dtype)
    oc_ref[...] = _dg(xb, wc_ref[...]).astype(oc_ref.dtype)


def _in_proj(h, g, w5, wg, wr, wc, tm=512):
    T = h.shape[0]
    ws = (w5, wg, wr, wc)
    row = lambda i: (i, 0)
    fixed = lambda i: (0, 0)
    return pl.pallas_call(
        _in_proj_kernel,
        grid=(T // tm,),
        in_specs=[pl.BlockSpec((tm, D_MODEL), row), pl.BlockSpec((1, D_MODEL), fixed)]
        + [pl.BlockSpec(w.shape, fixed) for w in ws],
        out_specs=[pl.BlockSpec((tm, w.shape[1]), row) for w in ws],
        out_shape=[jax.ShapeDtypeStruct((T, w.shape[1]), ACT) for w in ws],
        compiler_params=_cparams("parallel"),
        name="in_proj",
    )(h, g, *ws)


def _kv_proj_kernel(m_ref, g_ref, wk_ref, wv_ref, k_ref, v_ref):
    mb = _rms(m_ref[...], g_ref[...]).astype(BF16)
    k_ref[...] = _dg(mb, wk_ref[...]).astype(BF16)
    v_ref[...] = _dg(mb, wv_ref[...]).astype(BF16)


def _kv_proj(mem2d, g, wk, wv, tm=256):
    R = mem2d.shape[0]
    row = lambda i: (i, 0)
    fixed = lambda i: (0, 0)
    return pl.pallas_call(
        _kv_proj_kernel,
        grid=(R // tm,),
        in_specs=[pl.BlockSpec((tm, D_MODEL), row), pl.BlockSpec((1, D_MODEL), fixed),
                  pl.BlockSpec((D_MODEL, D_MODEL), fixed), pl.BlockSpec((D_MODEL, D_MODEL), fixed)],
        out_specs=[pl.BlockSpec((tm, D_MODEL), row)] * 2,
        out_shape=[jax.ShapeDtypeStruct((R, D_MODEL), BF16)] * 2,
        compiler_params=_cparams("parallel"),
        name="kv_proj",
    )(mem2d, g, wk, wv)


def _out_proj_body(y_refs, beta_ref, w_ref, h):
    ys = [y_ref[...].astype(F32) for y_ref in y_refs]
    mixed = jnp.concatenate(ys, axis=1) * beta_ref[...]
    return h + _dg(mixed.astype(BF16), w_ref[...])


def _final_norm_kernel(x_ref, g_ref, o_ref):
    o_ref[...] = _rms(x_ref[...], g_ref[...])


def _final_norm(h, g, tm=1024):
    T = h.shape[0]
    return pl.pallas_call(
        _final_norm_kernel,
        grid=(T // tm,),
        in_specs=[pl.BlockSpec((tm, D_MODEL), lambda i: (i, 0)), pl.BlockSpec((1, D_MODEL), lambda i: (0, 0))],
        out_specs=pl.BlockSpec((tm, D_MODEL), lambda i: (i, 0)),
        out_shape=jax.ShapeDtypeStruct((T, D_MODEL), F32),
        compiler_params=_cparams("parallel"),
        name="final_norm",
    )(h, g)


S5_BLOCKS = S5_NS // 128


def _s5_kernel(u_ref, bpad_ref, cpad_ref, lam_ref, d_ref, gw_ref, gb_ref, o_ref, xr_ref, xi_ref, carry_ref):
    nb, tl = u_ref.shape[0], u_ref.shape[1]
    nblk = S5_BLOCKS

    @pl.when(pl.program_id(0) == 0)
    def _():
        carry_ref[...] = jnp.zeros_like(carry_ref)

    for bi in range(nb):
        ub = u_ref[bi].astype(BF16)
        for s in range(nblk):
            bu = _dg(ub, bpad_ref[s])
            xr_ref[bi, pl.ds(s, tl, stride=nblk), :] = bu[:, 0:128]
            xi_ref[bi, pl.ds(s, tl, stride=nblk), :] = bu[:, 128:256]

    lam_re = lam_ref[0]
    lam_im = lam_ref[1]

    def step(t, carry):
        r0 = pl.multiple_of(t * nblk, nblk)
        out = []
        for bi in range(nb):
            c_re, c_im = carry[2 * bi], carry[2 * bi + 1]
            x_re = (lam_re * c_re - lam_im * c_im) + xr_ref[bi, pl.ds(r0, nblk), :]
            x_im = (lam_re * c_im + lam_im * c_re) + xi_ref[bi, pl.ds(r0, nblk), :]
            xr_ref[bi, pl.ds(r0, nblk), :] = x_re
            xi_ref[bi, pl.ds(r0, nblk), :] = x_im
            out += [x_re, x_im]
        return tuple(out)

    init = tuple(carry_ref[i] for i in range(2 * nb))
    last = lax.fori_loop(0, tl, step, init, unroll=8)
    for i in range(2 * nb):
        carry_ref[i] = last[i]

    for bi in range(nb):
        y = d_ref[...] * u_ref[bi].astype(F32)
        for s in range(nblk):
            st = jnp.concatenate([xr_ref[bi, pl.ds(s, tl, stride=nblk), :],
                                  xi_ref[bi, pl.ds(s, tl, stride=nblk), :]], axis=1)
            y = y + _dg(st.astype(BF16), cpad_ref[s])
        y = 0.5 * y * (1.0 + jnp.tanh(math.sqrt(2.0 / math.pi) * (y + 0.044715 * (y * y * y))))
        gate = _dg(y.astype(BF16), gw_ref[...]) + gb_ref[...]
        o_ref[bi] = (y * _sigmoid(gate)).astype(o_ref.dtype)


def _s5_mixer(p, prm, B, L, tl=256):
    rows = tl * S5_BLOCKS
    return pl.pallas_call(
        _s5_kernel,
        grid=(L // tl,),
        in_specs=[pl.BlockSpec((B, tl, W_GROUP), lambda l: (0, l, 0))]
        + [pl.BlockSpec(c.shape, lambda l, n=c.ndim: (0,) * n) for c in prm],
        out_specs=pl.BlockSpec((B, tl, W_GROUP), lambda l: (0, l, 0)),
        out_shape=jax.ShapeDtypeStruct((B, L, W_GROUP), ACT),
        scratch_shapes=[pltpu.VMEM((B, rows, 128), F32), pltpu.VMEM((B, rows, 128), F32),
                        pltpu.VMEM((2 * B, S5_BLOCKS, 128), F32)],
        compiler_params=_cparams("arbitrary"),
        name="s5_mixer",
    )(p.reshape(B, L, W_GROUP), *prm)


def _s5_params(lam_re, lam_im, b_re, b_im, c_re, c_im, d_skip, log_dt, glu_w, glu_b):
    G, N, P = S5_GROUPS, S5_STATE, S5_GROUP_CH
    dt = jnp.exp(log_dt)[:, None]
    mag = jnp.exp(lam_re * dt)
    lb_re = mag * jnp.cos(lam_im * dt)
    lb_im = mag * jnp.sin(lam_im * dt)
    den = lam_re * lam_re + lam_im * lam_im
    n_re, n_im = lb_re - 1.0, lb_im
    f_re = (n_re * lam_re + n_im * lam_im) / den
    f_im = (n_im * lam_re - n_re * lam_im) / den
    bb_re = f_re[:, :, None] * b_re - f_im[:, :, None] * b_im
    bb_im = f_re[:, :, None] * b_im + f_im[:, :, None] * b_re
    eye = jnp.eye(G, dtype=F32)
    bbd_re = jnp.einsum('gnp,gh->gphn', bb_re, eye).reshape(G * P, G * N)
    bbd_im = jnp.einsum('gnp,gh->gphn', bb_im, eye).reshape(G * P, G * N)
    cbd_re = jnp.einsum('gpn,gh->gnhp', c_re, eye).reshape(G * N, G * P)
    cbd_im = jnp.einsum('gpn,gh->gnhp', -c_im, eye).reshape(G * N, G * P)
    nblk = S5_BLOCKS
    bpad = jnp.concatenate([bbd_re.reshape(G * P, nblk, 128).transpose(1, 0, 2),
                            bbd_im.reshape(G * P, nblk, 128).transpose(1, 0, 2)], axis=2).astype(BF16)
    cpad = jnp.concatenate([cbd_re.reshape(nblk, 128, G * P), cbd_im.reshape(nblk, 128, G * P)],
                           axis=1).astype(BF16)
    lam_tab = jnp.stack([lb_re.reshape(nblk, 128), lb_im.reshape(nblk, 128)])
    return (bpad, cpad, lam_tab, d_skip.reshape(1, W_GROUP), glu_w.astype(BF16), glu_b.reshape(1, W_GROUP))


def _gla_kernel(p_ref, wup_ref, bup_ref, tri_ref, sege_ref, segv_ref, ng_ref, o_ref,
                st_ref, b_ref, oacc_ref):
    C = GLA_CHUNK
    nb, tl = p_ref.shape[0], p_ref.shape[1]

    @pl.when(pl.program_id(0) == 0)
    def _():
        st_ref[...] = jnp.zeros_like(st_ref)

    for bi in range(nb):
        x = _dg(p_ref[bi, :, 768:896].astype(BF16), wup_ref[...]) + bup_ref[...]
        log_alpha = (jnp.minimum(x, 0.0) - jnp.log1p(jnp.exp(-jnp.abs(x)))) * (1.0 / GLA_TAU)
        b_ref[bi] = _dot_exact_lhs(tri_ref[...], log_alpha)

    row_i = _iota((C, 128), 0)
    bd_mask = (_iota((256, 128), 0) >> 6) == (_iota((256, 128), 1) >> 5)
    sege = sege_ref[...]

    def chain(bi, r0):
        q = p_ref[bi, pl.ds(r0, C), 0:128].astype(F32) * (GLA_DK ** -0.5)
        k = p_ref[bi, pl.ds(r0, C), 128:256].astype(F32)
        v = p_ref[bi, pl.ds(r0, C), 256:512].astype(F32)
        b = b_ref[bi, pl.ds(r0, C), :]
        parts = []
        for j in range(C):
            e = jnp.exp(b - b[j:j + 1, :])
            parts.append(jnp.where(row_i >= j, (q * k[j:j + 1, :]) * e, 0.0).astype(BF16))
        t = jnp.concatenate(parts, axis=0)
        w = _dg(t, sege)
        st = st_ref[bi]
        qe = (q * jnp.exp(b)).astype(BF16)
        o_inter = _dg(qe, st.astype(BF16), _NT)
        b_last = b[C - 1:C, :]
        kt = (k * jnp.exp(b_last - b)).astype(BF16)
        upd = _dg(v.astype(BF16), kt, _TN)
        yield
        o = o_inter + w[0:C, :] * v[0:1, :]
        for j in range(1, C):
            o = o + w[j * C:(j + 1) * C, :] * v[j:j + 1, :]
        st_ref[bi] = st * jnp.exp(b_last) + jnp.where(bd_mask, upd, 0.0)
        oacc_ref[bi, pl.ds(r0, C), :] = o

    def chunk(c, _):
        r0 = pl.multiple_of(c * C, C)
        _round_robin([chain(bi, r0) for bi in range(nb)])
        return 0

    lax.fori_loop(0, tl // C, chunk, 0)

    for bi in range(nb):
        o = oacc_ref[bi]
        ms = _dot1(o * o, segv_ref[...]) * (1.0 / GLA_DV)
        g = p_ref[bi, :, 512:768].astype(F32)
        o_ref[bi] = (o * lax.rsqrt(ms + EPS) * ng_ref[...] * (g * _sigmoid(g))).astype(o_ref.dtype)


def _gla_mixer(p, prm, B, L, tl=256):
    wup, bup, ng = prm
    C = GLA_CHUNK
    r = jnp.arange(tl)
    tri = ((r[:, None] // C == r[None, :] // C) & (r[None, :] <= r[:, None])).astype(BF16)
    sege = (jnp.arange(128)[:, None] // GLA_DK == jnp.arange(256)[None, :] // GLA_DV).astype(BF16)
    segv = (jnp.arange(256)[:, None] // GLA_DV == jnp.arange(256)[None, :] // GLA_DV).astype(BF16)
    fixed = lambda l: (0, 0)
    return pl.pallas_call(
        _gla_kernel,
        grid=(L // tl,),
        in_specs=[pl.BlockSpec((B, tl, GLA_COLS_PAD), lambda l: (0, l, 0)),
                  pl.BlockSpec(wup.shape, fixed), pl.BlockSpec(bup.shape, fixed),
                  pl.BlockSpec(tri.shape, fixed), pl.BlockSpec(sege.shape, fixed),
                  pl.BlockSpec(segv.shape, fixed), pl.BlockSpec(ng.shape, fixed)],
        out_specs=pl.BlockSpec((B, tl, W_GROUP), lambda l: (0, l, 0)),
        out_shape=jax.ShapeDtypeStruct((B, L, W_GROUP), ACT),
        scratch_shapes=[pltpu.VMEM((B, 256, 128), F32), pltpu.VMEM((B, tl, 128), F32),
                        pltpu.VMEM((B, tl, 256), F32)],
        compiler_params=_cparams("arbitrary"),
        name="gla_mixer",
    )(p.reshape(B, L, GLA_COLS_PAD), wup, bup, tri, sege, segv, ng)


def _gla_params(w_up, b_up, norm_g):
    wup = jnp.zeros((128, 128), F32).at[:GLA_RANK, :].set(w_up).astype(BF16)
    return wup, b_up.reshape(1, 128), norm_g.reshape(1, W_GROUP)


def _rw_chunk(p, st_ref, prev_ref, o_ref, bi, prm):
    (mu, w0, w2, a0, a2, g2, kkg, ka, rk, lng, lnb, seg, tri) = prm
    C = RW_CHUNK
    W = W_GROUP
    prev_row = jnp.broadcast_to(prev_ref[bi, 0:1, :], p.shape)
    p_prev = _shift_rows(p, 1, prev_row)
    prev_ref[bi, 0:1, :] = p[C - 1:C, :]
    xm = p + (p_prev - p) * mu
    r = xm[:, 0:W]
    k = xm[:, W:2 * W]
    v = xm[:, 2 * W:3 * W]
    z = xm[:, 3 * W:3 * W + 128]

    w = -_softplus(-(w0 + _dg(jnp.tanh(z).astype(BF16), w2))) - 0.5
    logd = -jnp.exp(w)
    alr = _sigmoid(a0 + _dg(z.astype(BF16), a2))
    g = _dg(_sigmoid(z).astype(BF16), g2)
    lp = _dot_exact_lhs(tri, logd)
    yield

    kk = k * kkg
    kk = kk / jnp.maximum(jnp.sqrt(_dot1(kk * kk, seg)), 1e-12)
    k2 = k * (1.0 + (alr - 1.0) * ka)
    av = -kk
    bv = kk * alr
    bonus = _dot1(r * k2 * rk, seg) * v
    yield

    lp_c = lp[C - 1:C, :]
    e_neg = jnp.exp(-lp)
    e_rem = jnp.exp(lp_c - lp)
    ah = av * jnp.exp(lp - logd)
    rh = r * jnp.exp(lp)
    bh = bv * e_neg
    kh = k2 * e_neg
    b2 = bv * e_rem
    kc = k2 * e_rem
    p_c = jnp.exp(lp_c)

    lane_head = _iota((C, W), 1) >> 6
    hm = [lane_head == h for h in range(RW_HEADS)]
    zeros = jnp.zeros((C, W), F32)
    lhs = jnp.concatenate([jnp.where(m, ah, zeros) for m in hm] + [jnp.where(m, rh, zeros) for m in hm], axis=0)
    lhs = lhs.astype(BF16)
    xb = _dot1(lhs, bh, _NT).reshape(8, C, C)
    xk = _dot1(lhs, kh, _NT).reshape(8, C, C)
    yield

    ti = _iota((RW_HEADS, C, C), 1)
    tj = _iota((RW_HEADS, C, C), 2)
    strict = ti > tj
    incl = ti >= tj
    lm = jnp.where(strict, xb[0:4], 0.0)
    mak = jnp.where(strict, xk[0:4], 0.0)
    nrb = jnp.where(incl, xb[4:8], 0.0)
    nrk = jnp.where(incl, xk[4:8], 0.0)

    def bmm(a, b):
        return _dot1(a, b, _BNN)

    tinv = jnp.where(ti == tj, 1.0, 0.0) + lm
    lpow = lm
    for _ in range(5):
        lpow = bmm(lpow, lpow)
        yield
        tinv = tinv + bmm(tinv, lpow)
        yield

    def apply(m, x):
        full = _dot1(m.reshape(RW_HEADS * C, C), x).reshape(RW_HEADS, C, W)
        out = jnp.where(hm[0], full[0], zeros)
        for h in range(1, RW_HEADS):
            out = out + jnp.where(hm[h], full[h], zeros)
        return out

    tm = bmm(tinv, mak)
    a2h = apply(tinv, ah)
    nkv = apply(nrk, v)
    yield
    wv = apply(tm, v)
    r2 = rh + apply(nrb, a2h)
    bd = (_iota((W, W), 0) >> 6) == (_iota((W, W), 1) >> 6)
    eye = _iota((W, W), 0) == _iota((W, W), 1)
    a_c = jnp.where(eye, jnp.broadcast_to(p_c, (W, W)), 0.0) + jnp.where(bd, _dot1(b2, a2h, _TN), 0.0)
    yield
    y0 = apply(nrb, wv) + nkv
    g0 = jnp.where(bd, _dot1(b2, wv, _TN) + _dot1(kc, v, _TN), 0.0)
    s0 = st_ref[bi].astype(BF16)
    rs = _dot1(r2, s0)
    st_new = _dot1(a_c, s0)
    yield
    y = rs + y0
    st_ref[bi] = st_new + g0
    mean = _dot1(y, seg) * (1.0 / RW_N)
    yield
    yc = y - mean
    var = _dot1(yc * yc, seg) * (1.0 / RW_N)
    yield
    yn = yc * lax.rsqrt(var + RW_LN_EPS) * lng + lnb
    o_ref[bi] = ((yn + bonus) * g).astype(o_ref.dtype)


def _rw_kernel(p_ref, mu_ref, w0_ref, w2_ref, a0_ref, a2_ref, g2_ref, kk_ref, ka_ref, rk_ref,
               lng_ref, lnb_ref, seg_ref, tri_ref, o_ref, st_ref, prev_ref):
    @pl.when(pl.program_id(0) == 0)
    def _():
        st_ref[...] = jnp.zeros_like(st_ref)
        prev_ref[...] = jnp.zeros_like(prev_ref)

    prm = tuple(ref[...] for ref in (mu_ref, w0_ref, w2_ref, a0_ref, a2_ref, g2_ref, kk_ref, ka_ref, rk_ref,
                                     lng_ref, lnb_ref, seg_ref, tri_ref))
    _round_robin([_rw_chunk(p_ref[bi].astype(F32), st_ref, prev_ref, o_ref, bi, prm)
                  for bi in range(p_ref.shape[0])])


def _rw_mixer(p, prm, B, L):
    C = RW_CHUNK
    seg = (jnp.arange(256)[:, None] // RW_N == jnp.arange(256)[None, :] // RW_N).astype(BF16)
    tri = (jnp.arange(C)[None, :] <= jnp.arange(C)[:, None]).astype(BF16)
    fixed = lambda l: (0, 0)
    consts = tuple(prm) + (seg, tri)
    return pl.pallas_call(
        _rw_kernel,
        grid=(L // C,),
        in_specs=[pl.BlockSpec((B, C, RW_COLS), lambda l: (0, l, 0))]
        + [pl.BlockSpec(c.shape, fixed) for c in consts],
        out_specs=pl.BlockSpec((B, C, W_GROUP), lambda l: (0, l, 0)),
        out_shape=jax.ShapeDtypeStruct((B, L, W_GROUP), ACT),
        scratch_shapes=[pltpu.VMEM((B, W_GROUP, W_GROUP), F32), pltpu.VMEM((B, 8, RW_COLS), F32)],
        compiler_params=_cparams("arbitrary"),
        name="rwkv7_mixer",
    )(p.reshape(B, L, RW_COLS), *consts)


def _rw_params(mu, w0, w2, a0, a2, g2, k_k, k_a, r_k, ln_g, ln_b):
    row = lambda t: t.reshape(1, -1)
    w2p = jnp.zeros((128, W_GROUP), F32).at[0:RW_W_RANK].set(w2).astype(BF16)
    a2p = jnp.zeros((128, W_GROUP), F32).at[RW_W_RANK:RW_W_RANK + RW_A_RANK].set(a2).astype(BF16)
    g2p = jnp.zeros((128, W_GROUP), F32).at[RW_W_RANK + RW_A_RANK:].set(g2).astype(BF16)
    return (row(mu), row(w0), w2p, row(a0), a2p, g2p, row(k_k), row(k_a), row(r_k), row(ln_g), row(ln_b))


def _conv_kernel(p_ref, w_ref, b_ref, lg_ref, lb_ref, o_ref, u_ref, s_ref):
    tl = p_ref.shape[0]
    H = CONV_HALO
    RC = 64
    n = tl + H - SUBLANES

    @pl.when(pl.program_id(1) == 0)
    def _():
        u_ref[0:H, :] = jnp.zeros((H, W_GROUP), F32)

    u_ref[H:H + tl, :] = p_ref[:, 0:W_GROUP].astype(F32) * _sigmoid(p_ref[:, W_GROUP:2 * W_GROUP].astype(F32))
    for b in range(1, SUBLANES):
        for r0 in range(0, n, RC):
            rows = min(RC, n - r0)
            s_ref[b, r0:r0 + rows, :] = u_ref[r0 + b:r0 + b + rows, :]
    off = H - (CONV_WIDTH - 1)
    for c in range(tl // RC):
        acc = jnp.zeros((RC, W_GROUP), F32)
        for j in range(CONV_WIDTH):
            a, b = divmod(off + j, SUBLANES)
            lo = c * RC + a * SUBLANES
            tap = u_ref[lo:lo + RC, :] if b == 0 else s_ref[b, lo:lo + RC, :]
            acc = acc + w_ref[j:j + 1, :] * tap
        y = acc + b_ref[...]
        mu = jnp.mean(y, axis=-1, keepdims=True)
        yc = y - mu
        yn = yc * lax.rsqrt(jnp.mean(yc * yc, axis=-1, keepdims=True) + CONV_LN_EPS) * lg_ref[...] + lb_ref[...]
        o_ref[c * RC:(c + 1) * RC, :] = (yn * _sigmoid(yn)).astype(o_ref.dtype)
    tail = u_ref[tl:tl + H, :]
    u_ref[0:H, :] = tail


def _conv_mixer(p, prm, B, L, tl=512):
    w, b, lg, lb = prm
    fixed = lambda b_, l: (0, 0)
    return pl.pallas_call(
        _conv_kernel,
        grid=(B, L // tl),
        in_specs=[pl.BlockSpec((None, tl, 2 * W_GROUP), lambda b_, l: (b_, l, 0)),
                  pl.BlockSpec(w.shape, fixed), pl.BlockSpec(b.shape, fixed),
                  pl.BlockSpec(lg.shape, fixed), pl.BlockSpec(lb.shape, fixed)],
        out_specs=pl.BlockSpec((None, tl, W_GROUP), lambda b_, l: (b_, l, 0)),
        out_shape=jax.ShapeDtypeStruct((B, L, W_GROUP), ACT),
        scratch_shapes=[pltpu.VMEM((tl + CONV_HALO, W_GROUP), F32),
                        pltpu.VMEM((SUBLANES, tl + CONV_HALO - SUBLANES, W_GROUP), F32)],
        compiler_params=_cparams("parallel", "arbitrary"),
        name="conv_mixer",
    )(p.reshape(B, L, 2 * W_GROUP), w, b, lg, lb)


def _xattn_body(x, g_ref, wq_ref, k_ref, v_ref, wo_ref):
    q = _dg(_rms(x, g_ref[...]).astype(BF16), wq_ref[...]).astype(BF16)
    heads = [None] * X_HEADS

    def head(hd):
        lo = hd * X_HEAD_DIM
        s = _dg(q[:, lo:lo + X_HEAD_DIM], k_ref[:, lo:lo + X_HEAD_DIM], _NT) * (X_HEAD_DIM ** -0.5)
        yield
        e = jnp.exp(s - jnp.max(s, axis=-1, keepdims=True))
        pr = e / jnp.sum(e, axis=-1, keepdims=True)
        heads[hd] = _dg(pr.astype(BF16), v_ref[:, lo:lo + X_HEAD_DIM])

    _round_robin([head(hd) for hd in range(X_HEADS)])
    o = jnp.concatenate(heads, axis=1).astype(BF16)
    return x + _dg(o, wo_ref[...])


def _router_body(h, g_ref, wr_ref, br_ref, xn_ref, eid_ref, gate_ref):
    xn = _rms(h, g_ref[...])
    xn_ref[...] = xn
    lg = _dot1(wr_ref[...], xn, _NT) + br_ref[...]
    tm = lg.shape[1]
    gl = lg[0:8, :]
    sub = _iota((8, tm), 0)
    gmax = jnp.max(gl, axis=0, keepdims=True)
    gsel = jnp.min(jnp.where(gl == gmax, sub, 8), axis=0, keepdims=True)
    g_w = 1.0 / jnp.sum(jnp.exp(gl - gmax), axis=0, keepdims=True)
    e_in = jnp.zeros((8, tm), F32)
    for grp in range(N_GROUPS):
        e_in = e_in + jnp.where(gsel == grp, lg[8 + 8 * grp:16 + 8 * grp, :], 0.0)
    t1 = jnp.max(e_in, axis=0, keepdims=True)
    i1 = jnp.min(jnp.where(e_in == t1, sub, 8), axis=0, keepdims=True)
    rest = jnp.where(sub == i1, -jnp.inf, e_in)
    t2 = jnp.max(rest, axis=0, keepdims=True)
    i2 = jnp.min(jnp.where(rest == t2, sub, 8), axis=0, keepdims=True)
    e2 = jnp.exp(t2 - t1)
    p1 = 1.0 / (1.0 + e2)
    eid_ref[0:1, :] = gsel * EXP_PER_GROUP + i1
    eid_ref[1:2, :] = gsel * EXP_PER_GROUP + i2
    gate_ref[0:1, :] = p1 * g_w
    gate_ref[1:2, :] = (e2 * p1) * g_w


def _post_mix_kernel(y5_ref, yg_ref, yr_ref, yc_ref, beta_ref, wout_ref, h_ref, gx_ref, wq_ref, k_ref, v_ref,
                     wo_ref, gf_ref, wr_ref, br_ref, h2_ref, xn_ref, eid_ref, gate_ref):
    h1 = _out_proj_body((y5_ref, yg_ref, yr_ref, yc_ref), beta_ref, wout_ref, h_ref[...])
    h2 = _xattn_body(h1, gx_ref, wq_ref, k_ref, v_ref, wo_ref)
    h2_ref[...] = h2
    _router_body(h2, gf_ref, wr_ref, br_ref, xn_ref, eid_ref, gate_ref)


def _post_mix(ys, beta, w_out, h, gx, wq, k, v, wo, gf, wr, br, B, L, n_mem, tm=ROUTE_TILE):
    T = B * L
    nl = L // tm
    tile = lambda b, l: (b * nl + l, 0)
    tile3 = lambda b, l: (b * nl + l, 0, 0)
    fixed = lambda b, l: (0, 0)
    mem = lambda b, l: (b, 0, 0)
    mat = pl.BlockSpec((D_MODEL, D_MODEL), fixed)
    vec = pl.BlockSpec((1, D_MODEL), fixed)
    return pl.pallas_call(
        _post_mix_kernel,
        grid=(B, nl),
        in_specs=[pl.BlockSpec((tm, W_GROUP), tile)] * 4
        + [vec, mat, pl.BlockSpec((tm, D_MODEL), tile), vec, mat,
           pl.BlockSpec((None, n_mem, D_MODEL), mem), pl.BlockSpec((None, n_mem, D_MODEL), mem), mat, vec,
           pl.BlockSpec(wr.shape, fixed), pl.BlockSpec(br.shape, fixed)],
        out_specs=[pl.BlockSpec((tm, D_MODEL), tile), pl.BlockSpec((tm, D_MODEL), tile),
                   pl.BlockSpec((None, 2, tm), tile3), pl.BlockSpec((None, 2, tm), tile3)],
        out_shape=[jax.ShapeDtypeStruct((T, D_MODEL), F32), jax.ShapeDtypeStruct((T, D_MODEL), F32),
                   jax.ShapeDtypeStruct((T // tm, 2, tm), jnp.int32),
                   jax.ShapeDtypeStruct((T // tm, 2, tm), F32)],
        compiler_params=_cparams("parallel", "parallel"),
        name="post_mix",
    )(*ys, beta, w_out, h, gx, wq, k.reshape(B, n_mem, D_MODEL), v.reshape(B, n_mem, D_MODEL), wo, gf, wr, br)


def _rank_kernel(eid_ref, ut_ref, rank_ref, cnt_ref, carry_ref):
    tm = eid_ref.shape[1]

    @pl.when(pl.program_id(0) == 0)
    def _():
        carry_ref[...] = jnp.zeros_like(carry_ref)

    ex = _iota((N_EXPERTS, tm), 0)
    oh0 = jnp.where(ex == eid_ref[0:1, :], 1.0, 0.0)
    oh1 = jnp.where(ex == eid_ref[1:2, :], 1.0, 0.0)
    oh = oh0 + oh1
    cum = _dg(oh.astype(BF16), ut_ref[...])
    carry = carry_ref[:, 0:1]
    before = cum - oh + carry
    rank_ref[0:1, :] = jnp.sum(oh0 * before, axis=0, keepdims=True).astype(jnp.int32)
    rank_ref[1:2, :] = jnp.sum(oh1 * before, axis=0, keepdims=True).astype(jnp.int32)
    total = carry + cum[:, tm - 1:tm]
    carry_ref[...] = jnp.broadcast_to(total, carry_ref.shape)
    cnt_ref[...] = jnp.broadcast_to(total, cnt_ref.shape).astype(jnp.int32)


def _rank(eid, tm=ROUTE_TILE):
    nt = eid.shape[0]
    ut = (jnp.arange(tm)[:, None] <= jnp.arange(tm)[None, :]).astype(BF16)
    return pl.pallas_call(
        _rank_kernel,
        grid=(nt,),
        in_specs=[pl.BlockSpec((None, 2, tm), lambda i: (i, 0, 0)), pl.BlockSpec((tm, tm), lambda i: (0, 0))],
        out_specs=[pl.BlockSpec((None, 2, tm), lambda i: (i, 0, 0)),
                   pl.BlockSpec((N_EXPERTS, 128), lambda i: (0, 0))],
        out_shape=[jax.ShapeDtypeStruct((nt, 2, tm), jnp.int32),
                   jax.ShapeDtypeStruct((N_EXPERTS, 128), jnp.int32)],
        scratch_shapes=[pltpu.VMEM((N_EXPERTS, 128), F32)],
        compiler_params=_cparams("arbitrary"),
        name="moe_rank",
    )(eid, ut)


def _slot_kernel(ps_ref, eid_ref, rank_ref, slot_ref):
    eid = eid_ref[...]
    acc = rank_ref[...]
    for e in range(N_EXPERTS):
        acc = acc + jnp.where(eid == e, ps_ref[e], 0)
    slot_ref[...] = acc


def _slots(pad_starts, eid, rank):
    nt, _, tm = eid.shape
    shape2d = (nt * 2, tm)
    out = pl.pallas_call(
        _slot_kernel,
        grid_spec=pltpu.PrefetchScalarGridSpec(
            num_scalar_prefetch=1,
            grid=(1,),
            in_specs=[pl.BlockSpec(shape2d, lambda i, ps: (0, 0))] * 2,
            out_specs=pl.BlockSpec(shape2d, lambda i, ps: (0, 0)),
        ),
        out_shape=jax.ShapeDtypeStruct(shape2d, jnp.int32),
        compiler_params=_cparams("arbitrary"),
        name="moe_slots",
    )(pad_starts, eid.reshape(shape2d), rank.reshape(shape2d))
    return out.reshape(nt, 2, tm).transpose(0, 2, 1).reshape(nt * tm * 2)


def _dispatch_kernel(lo_ref, np_ref, nu_ref, slot_ref, x_ref, xs_ref, z_ref, sem):
    groups = x_ref.shape[0]
    first = pl.program_id(0) == 0
    n_blocks = xs_ref.shape[0] // MOE_BLOCK

    def pad_copy(e, r):
        return pltpu.make_async_copy(z_ref.at[pl.ds(0, 1)], xs_ref.at[pl.ds(lo_ref[e] + r, 1)], sem)

    def tail_copy(blk):
        rows = pl.ds(pl.multiple_of(blk * MOE_BLOCK, MOE_BLOCK), MOE_BLOCK)
        return pltpu.make_async_copy(z_ref, xs_ref.at[rows], sem)

    def zero_fill(act):
        def per_expert(e, c):
            def row(r, c2):
                act(pad_copy(e, r))
                return c2
            return lax.fori_loop(0, np_ref[e], row, c)

        def tail(blk, c):
            act(tail_copy(blk))
            return c

        lax.fori_loop(0, N_EXPERTS, per_expert, 0)
        lax.fori_loop(nu_ref[0], n_blocks, tail, 0)

    @pl.when(first)
    def _():
        z_ref[...] = jnp.zeros_like(z_ref)
        zero_fill(lambda cp: cp.start())

    def issue(i, _):
        for u in range(SUBLANES):
            for k in range(2):
                s = slot_ref[i * (2 * SUBLANES) + (2 * u + k)]
                pltpu.make_async_copy(x_ref.at[i, pl.ds(u, 1)], xs_ref.at[pl.ds(s, 1)], sem).start(priority=k)
        return 0

    lax.fori_loop(0, groups, issue, 0)
    for _ in range(2):
        pltpu.make_async_copy(x_ref, x_ref, sem).wait()

    @pl.when(first)
    def _():
        zero_fill(lambda cp: cp.wait())


def _dispatch(pad_lo, n_pad, n_used, slot, xn, n_slots, tm=ROUTE_TILE):
    T = xn.shape[0]
    g = tm // SUBLANES
    return pl.pallas_call(
        _dispatch_kernel,
        grid_spec=pltpu.PrefetchScalarGridSpec(
            num_scalar_prefetch=3,
            grid=(T // tm,),
            in_specs=[pl.BlockSpec((2 * tm,), lambda i, lo, npad, nu: (i,), memory_space=pltpu.SMEM),
                      pl.BlockSpec((g, SUBLANES, D_MODEL), lambda i, lo, npad, nu: (i, 0, 0))],
            out_specs=pl.BlockSpec(memory_space=pl.ANY),
            scratch_shapes=[pltpu.VMEM((MOE_BLOCK, D_MODEL), F32), pltpu.SemaphoreType.DMA(())],
        ),
        out_shape=jax.ShapeDtypeStruct((n_slots, D_MODEL), F32),
        compiler_params=_cparams("arbitrary"),
        name="moe_dispatch",
    )(pad_lo, n_pad, n_used, slot, xn.reshape(T // SUBLANES, SUBLANES, D_MODEL))


def _expert_kernel(be_ref, nu_ref, x_ref, wg_ref, wu_ref, wd_ref, y_ref, wgb_ref, wub_ref, wdb_ref):
    i = pl.program_id(0)
    used = i < nu_ref[0]

    @pl.when(jnp.logical_and(used, jnp.logical_or(i == 0, be_ref[i] != be_ref[jnp.maximum(i - 1, 0)])))
    def _():
        wgb_ref[...] = wg_ref[...].astype(BF16)
        wub_ref[...] = wu_ref[...].astype(BF16)
        wdb_ref[...] = wd_ref[...].astype(BF16)

    @pl.when(used)
    def _():
        xb = x_ref[...].astype(BF16)
        gate = _dg(xb, wgb_ref[...])
        hid = (gate * _sigmoid(gate)) * _dg(xb, wub_ref[...])
        y_ref[...] = _dg(hid.astype(BF16), wdb_ref[...])

    @pl.when(jnp.logical_not(used))
    def _():
        y_ref[...] = jnp.zeros_like(y_ref)


def _experts(block_exp, n_used, xs, w_gate, w_up, w_down, layer):
    n_slots = xs.shape[0]
    nb = n_slots // MOE_BLOCK
    last = lambda i, nu: jnp.minimum(i, nu[0] - 1)
    wmap = lambda i, be, nu: (layer, be[last(i, nu)], 0, 0)
    return pl.pallas_call(
        _expert_kernel,
        grid_spec=pltpu.PrefetchScalarGridSpec(
            num_scalar_prefetch=2,
            grid=(nb,),
            in_specs=[pl.BlockSpec((MOE_BLOCK, D_MODEL), lambda i, be, nu: (last(i, nu), 0)),
                      pl.BlockSpec((None, None, D_MODEL, D_EXPERT), wmap),
                      pl.BlockSpec((None, None, D_MODEL, D_EXPERT), wmap),
                      pl.BlockSpec((None, None, D_EXPERT, D_MODEL), wmap)],
            out_specs=pl.BlockSpec((MOE_BLOCK, D_MODEL), lambda i, be, nu: (i, 0)),
            scratch_shapes=[pltpu.VMEM((D_MODEL, D_EXPERT), BF16), pltpu.VMEM((D_MODEL, D_EXPERT), BF16),
                            pltpu.VMEM((D_EXPERT, D_MODEL), BF16)],
        ),
        out_shape=jax.ShapeDtypeStruct((n_slots, D_MODEL), F32),
        compiler_params=_cparams("arbitrary"),
        name="moe_experts",
    )(block_exp, n_used, xs, w_gate, w_up, w_down)


def _combine_kernel(slot_ref, gate_ref, h_ref, ys_ref, o_ref, y0_ref, y1_ref, sem):
    tm = h_ref.shape[0]
    groups = tm // SUBLANES
    bufs = (y0_ref, y1_ref)

    def issue(i, _):
        for u in range(SUBLANES):
            for k in range(2):
                s = slot_ref[i * (2 * SUBLANES) + (2 * u + k)]
                pltpu.make_async_copy(ys_ref.at[pl.ds(s, 1)], bufs[k].at[i, pl.ds(u, 1)], sem).start(priority=k)
        return 0

    lax.fori_loop(0, groups, issue, 0)
    for k in range(2):
        pltpu.make_async_copy(bufs[k], bufs[k], sem).wait()
    y0 = y0_ref[...].reshape(tm, D_MODEL)
    y1 = y1_ref[...].reshape(tm, D_MODEL)
    o_ref[...] = h_ref[...] + gate_ref[:, 0:1] * y0 + gate_ref[:, 1:2] * y1


def _combine(slot, gate_t, h, ys, tm=ROUTE_TILE):
    T = h.shape[0]
    g = tm // SUBLANES
    return pl.pallas_call(
        _combine_kernel,
        grid=(T // tm,),
        in_specs=[pl.BlockSpec((2 * tm,), lambda i: (i,), memory_space=pltpu.SMEM),
                  pl.BlockSpec((tm, 2), lambda i: (i, 0)),
                  pl.BlockSpec((tm, D_MODEL), lambda i: (i, 0)),
                  pl.BlockSpec(memory_space=pl.ANY)],
        out_specs=pl.BlockSpec((tm, D_MODEL), lambda i: (i, 0)),
        scratch_shapes=[pltpu.VMEM((g, SUBLANES, D_MODEL), F32), pltpu.VMEM((g, SUBLANES, D_MODEL), F32),
                        pltpu.SemaphoreType.DMA(())],
        out_shape=jax.ShapeDtypeStruct((T, D_MODEL), F32),
        compiler_params=_cparams("arbitrary"),
        name="moe_combine",
    )(slot, gate_t, h, ys)


def _router_params(group_w, group_b, expert_w, expert_b):
    wr = jnp.zeros((ROUTE_ROWS, D_MODEL), F32).at[0:N_GROUPS].set(group_w.T).at[8:].set(expert_w.T)
    br = jnp.full((ROUTE_ROWS, 1), -1e30, F32).at[0:N_GROUPS, 0].set(group_b).at[8:, 0].set(expert_b)
    return wr, br


def _moe(h, xn, eid, gate, w_gate, w_up, w_down, layer):
    T = h.shape[0]
    rank, cnt = _rank(eid)
    counts = cnt[:, 0]
    padded = (counts + MOE_BLOCK - 1) // MOE_BLOCK * MOE_BLOCK
    pad_ends = jnp.cumsum(padded)
    pad_starts = (pad_ends - padded).astype(jnp.int32)
    n_slots = ((T * 2 + MOE_BLOCK - 1) // MOE_BLOCK + N_EXPERTS) * MOE_BLOCK
    block_start = jnp.arange(n_slots // MOE_BLOCK, dtype=jnp.int32) * MOE_BLOCK
    owner = jnp.sum((pad_ends[None, :] <= block_start[:, None]).astype(jnp.int32), axis=1)
    block_exp = jnp.minimum(owner, N_EXPERTS - 1).astype(jnp.int32)
    slot = _slots(pad_starts, eid, rank)
    n_used = (pad_ends[N_EXPERTS - 1:] // MOE_BLOCK).astype(jnp.int32)
    xs = _dispatch((pad_starts + counts).astype(jnp.int32), (padded - counts).astype(jnp.int32), n_used,
                   slot, xn, n_slots)
    ys = _experts(block_exp, n_used, xs, w_gate, w_up, w_down, layer)
    gate_t = gate.transpose(0, 2, 1).reshape(T, 2)
    return _combine(slot, gate_t, h, ys)


def kernel(x, mem, norm_mix_g, w_in, w_out, mix_beta, s5_lam_re, s5_lam_im, s5_b_re, s5_b_im, s5_c_re, s5_c_im, s5_d, s5_log_dt, s5_glu_w, s5_glu_b, gla_w_up, gla_b_up, gla_norm_g, rw_mu, rw_w0, rw_w2, rw_a0, rw_a2, rw_g2, rw_k_k, rw_k_a, rw_r_k, rw_ln_g, rw_ln_b, conv_w, conv_b, conv_ln_g, conv_ln_b, norm_xattn_g, norm_mem_g, xa_wq, xa_wk, xa_wv, xa_wo, norm_ffn_g, moe_group_w, moe_group_b, moe_expert_w, moe_expert_b, moe_w_gate, moe_w_up, moe_w_down, norm_final_g):
    B, L, D = x.shape
    n_mem = mem.shape[1]
    depth = w_in.shape[0]
    T = B * L
    h = x.reshape(T, D)
    mem2d = mem.reshape(B * n_mem, D)
    row = lambda t: t.reshape(1, -1)
    c0 = W_GROUP
    c1 = c0 + 784
    c2 = c1 + RW_COLS
    for l in range(depth):
        wl = w_in[l]
        w5 = wl[:, :c0].astype(BF16)
        wg = jnp.pad(wl[:, c0:c1], ((0, 0), (0, GLA_COLS_PAD - 784))).astype(BF16)
        wr = wl[:, c1:c2].astype(BF16)
        wc = wl[:, c2:].astype(BF16)
        p5, pg, pr, pc = _in_proj(h, row(norm_mix_g[l]), w5, wg, wr, wc)
        y5 = _s5_mixer(p5, _s5_params(s5_lam_re[l], s5_lam_im[l], s5_b_re[l], s5_b_im[l], s5_c_re[l], s5_c_im[l],
                                      s5_d[l], s5_log_dt[l], s5_glu_w[l], s5_glu_b[l]), B, L)
        yg = _gla_mixer(pg, _gla_params(gla_w_up[l], gla_b_up[l], gla_norm_g[l]), B, L)
        yr = _rw_mixer(pr, _rw_params(rw_mu[l], rw_w0[l], rw_w2[l], rw_a0[l], rw_a2[l], rw_g2[l], rw_k_k[l],
                                      rw_k_a[l], rw_r_k[l], rw_ln_g[l], rw_ln_b[l]), B, L)
        yc = _conv_mixer(pc, (conv_w[l], row(conv_b[l]), row(conv_ln_g[l]), row(conv_ln_b[l])), B, L)
        ys = [t.reshape(T, W_GROUP) for t in (y5, yg, yr, yc)]
        kmem, vmem = _kv_proj(mem2d, row(norm_mem_g[l]), xa_wk[l].astype(BF16), xa_wv[l].astype(BF16))
        wr, br = _router_params(moe_group_w[l], moe_group_b[l], moe_expert_w[l], moe_expert_b[l])
        h, xn, eid, gate = _post_mix(ys, row(mix_beta[l]), w_out[l].astype(BF16), h, row(norm_xattn_g[l]),
                                     xa_wq[l].astype(BF16), kmem, vmem, xa_wo[l].astype(BF16),
                                     row(norm_ffn_g[l]), wr, br, B, L, n_mem)
        h = _moe(h, xn, eid, gate, moe_w_gate, moe_w_up, moe_w_down, l)
    return _final_norm(h, row(norm_final_g)).reshape(B, L, D)
```

```python
import functools
import math

import jax
import jax.numpy as jnp
from jax import lax
from jax.experimental import pallas as pl
from jax.experimental.pallas import tpu as pltpu

F32 = jnp.float32
BF16 = jnp.bfloat16
ACT = jnp.bfloat16

D_MODEL = 1024
W_GROUP = 256
EPS = 1e-6

S5_GROUP_CH = 16
S5_GROUPS = 16
S5_STATE = 64
S5_NS = S5_GROUPS * S5_STATE

GLA_HEADS = 4
GLA_DV = 64
GLA_DK = 32
GLA_RANK = 16
GLA_TAU = 16.0
GLA_CHUNK = 16
GLA_COLS_PAD = 896

RW_HEADS = 4
RW_N = 64
RW_W_RANK = 32
RW_A_RANK = 32
RW_G_RANK = 64
RW_LN_EPS = 64e-5
RW_COLS = 896
RW_CHUNK = 64

CONV_WIDTH = 31
CONV_LN_EPS = 1e-5
CONV_HALO = 32

X_HEADS = 4
X_HEAD_DIM = 256

N_GROUPS = 4
EXP_PER_GROUP = 8
N_EXPERTS = 32
D_EXPERT = 512
MOE_BLOCK = 256
ROUTE_ROWS = 40
ROUTE_TILE = 512

SUBLANES = 8
VMEM_LIMIT = 56 * 1024 * 1024


def _cparams(*sem):
    return pltpu.CompilerParams(dimension_semantics=sem, vmem_limit_bytes=VMEM_LIMIT)


_NN = (((1,), (0,)), ((), ()))
_NT = (((1,), (1,)), ((), ()))
_TN = (((0,), (0,)), ((), ()))
_BNN = (((2,), (1,)), ((0,), (0,)))


def _dg(a, b, dims=_NN):
    return lax.dot_general(a, b, dims, preferred_element_type=F32)


def _split2(a):
    hi = a.astype(BF16)
    lo = (a - hi.astype(F32)).astype(BF16)
    return hi, lo


def _split3(a):
    hi = a.astype(BF16)
    r1 = a - hi.astype(F32)
    mid = r1.astype(BF16)
    lo = (r1 - mid.astype(F32)).astype(BF16)
    return hi, mid, lo


def _dot1(a, b, dims=_NN):
    return _dg(a.astype(BF16), b.astype(BF16), dims)


def _dot3(a, b, dims=_NN):
    ah, al = _split2(a)
    bh, bl = _split2(b)
    return _dg(ah, bh, dims) + (_dg(ah, bl, dims) + _dg(al, bh, dims))


def _dot_exact_rhs(a, b_bf16, dims=_NN):
    ah, am, al = _split3(a)
    return _dg(ah, b_bf16, dims) + (_dg(am, b_bf16, dims) + _dg(al, b_bf16, dims))


def _dot_exact_lhs(a_bf16, b, dims=_NN):
    bh, bm, bl = _split3(b)
    return _dg(a_bf16, bh, dims) + (_dg(a_bf16, bm, dims) + _dg(a_bf16, bl, dims))


def _iota(shape, dim):
    return lax.broadcasted_iota(jnp.int32, shape, dim)


def _rms(x, g):
    return x * lax.rsqrt(jnp.mean(x * x, axis=-1, keepdims=True) + EPS) * g


def _sigmoid(x):
    return 1.0 / (1.0 + jnp.exp(-x))


def _softplus(x):
    return jnp.maximum(x, 0.0) + jnp.log1p(jnp.exp(-jnp.abs(x)))


D_PACK = D_MODEL // 2
U32 = jnp.uint32


def _pack_rows(x):
    bits = lax.bitcast_convert_type(x.astype(BF16).astype(F32), U32)
    return (bits[:, :D_PACK] >> 16) | bits[:, D_PACK:]


def _unpack_rows(w):
    lo = lax.bitcast_convert_type(w << 16, F32)
    hi = lax.bitcast_convert_type(w & jnp.uint32(0xFFFF0000), F32)
    return jnp.concatenate([lo, hi], axis=1)


def _round_robin(chains):
    done = object()
    while chains:
        chains = [c for c in chains if next(c, done) is not done]


def _shift_rows(x, s, fill=None):
    rolled = pltpu.roll(x, s, 0)
    rows = _iota(x.shape, 0)
    if fill is None:
        fill = jnp.zeros_like(x)
    return jnp.where(rows >= s, rolled, fill)


def _in_proj_kernel(x_ref, g_ref, w5_ref, wg_ref, wr_ref, wc_ref, o5_ref, og_ref, or_ref, oc_ref):
    xb = _rms(x_ref[...], g_ref[...]).astype(BF16)
    o5_ref[...] = _dg(xb, w5_ref[...]).astype(o5_ref.dtype)
    og_ref[...] = _dg(xb, wg_ref[...]).astype(og_ref.dtype)
    or_ref[...] = _dg(xb, wr_ref[...]).astype(or_ref.dtype)
    oc_ref[...] = _dg(xb, wc_ref[...]).astype(oc_ref.dtype)


def _in_proj(h, g, w5, wg, wr, wc, tm=512):
    T = h.shape[0]
    ws = (w5, wg, wr, wc)
    row = lambda i: (i, 0)
    fixed = lambda i: (0, 0)
    return pl.pallas_call(
        _in_proj_kernel,
        grid=(T // tm,),
        in_specs=[pl.BlockSpec((tm, D_MODEL), row), pl.BlockSpec((1, D_MODEL), fixed)]
        + [pl.BlockSpec(w.shape, fixed) for w in ws],
        out_specs=[pl.BlockSpec((tm, w.shape[1]), row) for w in ws],
        out_shape=[jax.ShapeDtypeStruct((T, w.shape[1]), ACT) for w in ws],
        compiler_params=_cparams("parallel"),
        name="in_proj",
    )(h, g, *ws)


def _kv_proj_kernel(m_ref, g_ref, wk_ref, wv_ref, k_ref, v_ref):
    mb = _rms(m_ref[...], g_ref[...]).astype(BF16)
    k_ref[...] = _dg(mb, wk_ref[...]).astype(BF16)
    v_ref[...] = _dg(mb, wv_ref[...]).astype(BF16)


def _kv_proj(mem2d, g, wk, wv, tm=256):
    R = mem2d.shape[0]
    row = lambda i: (i, 0)
    fixed = lambda i: (0, 0)
    return pl.pallas_call(
        _kv_proj_kernel,
        grid=(R // tm,),
        in_specs=[pl.BlockSpec((tm, D_MODEL), row), pl.BlockSpec((1, D_MODEL), fixed),
                  pl.BlockSpec((D_MODEL, D_MODEL), fixed), pl.BlockSpec((D_MODEL, D_MODEL), fixed)],
        out_specs=[pl.BlockSpec((tm, D_MODEL), row)] * 2,
        out_shape=[jax.ShapeDtypeStruct((R, D_MODEL), BF16)] * 2,
        compiler_params=_cparams("parallel"),
        name="kv_proj",
    )(mem2d, g, wk, wv)


def _out_proj_body(y_refs, beta_ref, w_ref, h):
    ys = [y_ref[...].astype(F32) for y_ref in y_refs]
    mixed = jnp.concatenate(ys, axis=1) * beta_ref[...]
    return h + _dg(mixed.astype(BF16), w_ref[...])


def _final_norm_kernel(x_ref, g_ref, o_ref):
    o_ref[...] = _rms(x_ref[...], g_ref[...])


def _final_norm(h, g, tm=1024):
    T = h.shape[0]
    return pl.pallas_call(
        _final_norm_kernel,
        grid=(T // tm,),
        in_specs=[pl.BlockSpec((tm, D_MODEL), lambda i: (i, 0)), pl.BlockSpec((1, D_MODEL), lambda i: (0, 0))],
        out_specs=pl.BlockSpec((tm, D_MODEL), lambda i: (i, 0)),
        out_shape=jax.ShapeDtypeStruct((T, D_MODEL), F32),
        compiler_params=_cparams("parallel"),
        name="final_norm",
    )(h, g)


S5_BLOCKS = S5_NS // 128


def _s5_kernel(u_ref, bpad_ref, cpad_ref, lam_ref, d_ref, gw_ref, gb_ref, o_ref, xr_ref, xi_ref, carry_ref):
    nb, tl = u_ref.shape[0], u_ref.shape[1]
    nblk = S5_BLOCKS

    @pl.when(pl.program_id(0) == 0)
    def _():
        carry_ref[...] = jnp.zeros_like(carry_ref)

    for bi in range(nb):
        ub = u_ref[bi].astype(BF16)
        for s in range(nblk):
            bu = _dg(ub, bpad_ref[s])
            xr_ref[bi, pl.ds(s, tl, stride=nblk), :] = bu[:, 0:128]
            xi_ref[bi, pl.ds(s, tl, stride=nblk), :] = bu[:, 128:256]

    lam_re = lam_ref[0]
    lam_im = lam_ref[1]

    def step(t, carry):
        r0 = pl.multiple_of(t * nblk, nblk)
        out = []
        for bi in range(nb):
            c_re, c_im = carry[2 * bi], carry[2 * bi + 1]
            x_re = (lam_re * c_re - lam_im * c_im) + xr_ref[bi, pl.ds(r0, nblk), :]
            x_im = (lam_re * c_im + lam_im * c_re) + xi_ref[bi, pl.ds(r0, nblk), :]
            xr_ref[bi, pl.ds(r0, nblk), :] = x_re
            xi_ref[bi, pl.ds(r0, nblk), :] = x_im
            out += [x_re, x_im]
        return tuple(out)

    init = tuple(carry_ref[i] for i in range(2 * nb))
    last = lax.fori_loop(0, tl, step, init, unroll=8)
    for i in range(2 * nb):
        carry_ref[i] = last[i]

    for bi in range(nb):
        y = d_ref[...] * u_ref[bi].astype(F32)
        for s in range(nblk):
            st = jnp.concatenate([xr_ref[bi, pl.ds(s, tl, stride=nblk), :],
                                  xi_ref[bi, pl.ds(s, tl, stride=nblk), :]], axis=1)
            y = y + _dg(st.astype(BF16), cpad_ref[s])
        y = 0.5 * y * (1.0 + jnp.tanh(math.sqrt(2.0 / math.pi) * (y + 0.044715 * (y * y * y))))
        gate = _dg(y.astype(BF16), gw_ref[...]) + gb_ref[...]
        o_ref[bi] = (y * _sigmoid(gate)).astype(o_ref.dtype)


def _s5_mixer(p, prm, B, L, tl=256):
    rows = tl * S5_BLOCKS
    return pl.pallas_call(
        _s5_kernel,
        grid=(L // tl,),
        in_specs=[pl.BlockSpec((B, tl, W_GROUP), lambda l: (0, l, 0))]
        + [pl.BlockSpec(c.shape, lambda l, n=c.ndim: (0,) * n) for c in prm],
        out_specs=pl.BlockSpec((B, tl, W_GROUP), lambda l: (0, l, 0)),
        out_shape=jax.ShapeDtypeStruct((B, L, W_GROUP), ACT),
        scratch_shapes=[pltpu.VMEM((B, rows, 128), F32), pltpu.VMEM((B, rows, 128), F32),
                        pltpu.VMEM((2 * B, S5_BLOCKS, 128), F32)],
        compiler_params=_cparams("arbitrary"),
        name="s5_mixer",
    )(p.reshape(B, L, W_GROUP), *prm)


def _s5_params(lam_re, lam_im, b_re, b_im, c_re, c_im, d_skip, log_dt, glu_w, glu_b):
    G, N, P = S5_GROUPS, S5_STATE, S5_GROUP_CH
    dt = jnp.exp(log_dt)[:, None]
    mag = jnp.exp(lam_re * dt)
    lb_re = mag * jnp.cos(lam_im * dt)
    lb_im = mag * jnp.sin(lam_im * dt)
    den = lam_re * lam_re + lam_im * lam_im
    n_re, n_im = lb_re - 1.0, lb_im
    f_re = (n_re * lam_re + n_im * lam_im) / den
    f_im = (n_im * lam_re - n_re * lam_im) / den
    bb_re = f_re[:, :, None] * b_re - f_im[:, :, None] * b_im
    bb_im = f_re[:, :, None] * b_im + f_im[:, :, None] * b_re
    eye = jnp.eye(G, dtype=F32)
    bbd_re = jnp.einsum('gnp,gh->gphn', bb_re, eye).reshape(G * P, G * N)
    bbd_im = jnp.einsum('gnp,gh->gphn', bb_im, eye).reshape(G * P, G * N)
    cbd_re = jnp.einsum('gpn,gh->gnhp', c_re, eye).reshape(G * N, G * P)
    cbd_im = jnp.einsum('gpn,gh->gnhp', -c_im, eye).reshape(G * N, G * P)
    nblk = S5_BLOCKS
    bpad = jnp.concatenate([bbd_re.reshape(G * P, nblk, 128).transpose(1, 0, 2),
                            bbd_im.reshape(G * P, nblk, 128).transpose(1, 0, 2)], axis=2).astype(BF16)
    cpad = jnp.concatenate([cbd_re.reshape(nblk, 128, G * P), cbd_im.reshape(nblk, 128, G * P)],
                           axis=1).astype(BF16)
    lam_tab = jnp.stack([lb_re.reshape(nblk, 128), lb_im.reshape(nblk, 128)])
    return (bpad, cpad, lam_tab, d_skip.reshape(1, W_GROUP), glu_w.astype(BF16), glu_b.reshape(1, W_GROUP))


def _gla_kernel(p_ref, wup_ref, bup_ref, tri_ref, sege_ref, segv_ref, ng_ref, o_ref,
                st_ref, b_ref, oacc_ref):
    C = GLA_CHUNK
    nb, tl = p_ref.shape[0], p_ref.shape[1]

    @pl.when(pl.program_id(0) == 0)
    def _():
        st_ref[...] = jnp.zeros_like(st_ref)

    for bi in range(nb):
        x = _dg(p_ref[bi, :, 768:896].astype(BF16), wup_ref[...]) + bup_ref[...]
        log_alpha = (jnp.minimum(x, 0.0) - jnp.log1p(jnp.exp(-jnp.abs(x)))) * (1.0 / GLA_TAU)
        b_ref[bi] = _dot_exact_lhs(tri_ref[...], log_alpha)

    row_i = _iota((C, 128), 0)
    bd_mask = (_iota((256, 128), 0) >> 6) == (_iota((256, 128), 1) >> 5)
    sege = sege_ref[...]

    def chain(bi, r0):
        q = p_ref[bi, pl.ds(r0, C), 0:128].astype(F32) * (GLA_DK ** -0.5)
        k = p_ref[bi, pl.ds(r0, C), 128:256].astype(F32)
        v = p_ref[bi, pl.ds(r0, C), 256:512].astype(F32)
        b = b_ref[bi, pl.ds(r0, C), :]
        parts = []
        for j in range(C):
            e = jnp.exp(b - b[j:j + 1, :])
            parts.append(jnp.where(row_i >= j, (q * k[j:j + 1, :]) * e, 0.0).astype(BF16))
        t = jnp.concatenate(parts, axis=0)
        w = _dg(t, sege)
        st = st_ref[bi]
        qe = (q * jnp.exp(b)).astype(BF16)
        o_inter = _dg(qe, st.astype(BF16), _NT)
        b_last = b[C - 1:C, :]
        kt = (k * jnp.exp(b_last - b)).astype(BF16)
        upd = _dg(v.astype(BF16), kt, _TN)
        yield
        o = o_inter + w[0:C, :] * v[0:1, :]
        for j in range(1, C):
            o = o + w[j * C:(j + 1) * C, :] * v[j:j + 1, :]
        st_ref[bi] = st * jnp.exp(b_last) + jnp.where(bd_mask, upd, 0.0)
        oacc_ref[bi, pl.ds(r0, C), :] = o

    def chunk(c, _):
        r0 = pl.multiple_of(c * C, C)
        _round_robin([chain(bi, r0) for bi in range(nb)])
        return 0

    lax.fori_loop(0, tl // C, chunk, 0)

    for bi in range(nb):
        o = oacc_ref[bi]
        ms = _dot1(o * o, segv_ref[...]) * (1.0 / GLA_DV)
        g = p_ref[bi, :, 512:768].astype(F32)
        o_ref[bi] = (o * lax.rsqrt(ms + EPS) * ng_ref[...] * (g * _sigmoid(g))).astype(o_ref.dtype)


def _gla_mixer(p, prm, B, L, tl=256):
    wup, bup, ng = prm
    C = GLA_CHUNK
    r = jnp.arange(tl)
    tri = ((r[:, None] // C == r[None, :] // C) & (r[None, :] <= r[:, None])).astype(BF16)
    sege = (jnp.arange(128)[:, None] // GLA_DK == jnp.arange(256)[None, :] // GLA_DV).astype(BF16)
    segv = (jnp.arange(256)[:, None] // GLA_DV == jnp.arange(256)[None, :] // GLA_DV).astype(BF16)
    fixed = lambda l: (0, 0)
    return pl.pallas_call(
        _gla_kernel,
        grid=(L // tl,),
        in_specs=[pl.BlockSpec((B, tl, GLA_COLS_PAD), lambda l: (0, l, 0)),
                  pl.BlockSpec(wup.shape, fixed), pl.BlockSpec(bup.shape, fixed),
                  pl.BlockSpec(tri.shape, fixed), pl.BlockSpec(sege.shape, fixed),
                  pl.BlockSpec(segv.shape, fixed), pl.BlockSpec(ng.shape, fixed)],
        out_specs=pl.BlockSpec((B, tl, W_GROUP), lambda l: (0, l, 0)),
        out_shape=jax.ShapeDtypeStruct((B, L, W_GROUP), ACT),
        scratch_shapes=[pltpu.VMEM((B, 256, 128), F32), pltpu.VMEM((B, tl, 128), F32),
                        pltpu.VMEM((B, tl, 256), F32)],
        compiler_params=_cparams("arbitrary"),
        name="gla_mixer",
    )(p.reshape(B, L, GLA_COLS_PAD), wup, bup, tri, sege, segv, ng)


def _gla_params(w_up, b_up, norm_g):
    wup = jnp.zeros((128, 128), F32).at[:GLA_RANK, :].set(w_up).astype(BF16)
    return wup, b_up.reshape(1, 128), norm_g.reshape(1, W_GROUP)


def _rw_chunk(p, st_ref, prev_ref, o_ref, bi, prm):
    (mu, w0, w2, a0, a2, g2, kkg, ka, rk, lng, lnb, seg, tri) = prm
    C = RW_CHUNK
    W = W_GROUP
    prev_row = jnp.broadcast_to(prev_ref[bi, 0:1, :], p.shape)
    p_prev = _shift_rows(p, 1, prev_row)
    prev_ref[bi, 0:1, :] = p[C - 1:C, :]
    xm = p + (p_prev - p) * mu
    r = xm[:, 0:W]
    k = xm[:, W:2 * W]
    v = xm[:, 2 * W:3 * W]
    z = xm[:, 3 * W:3 * W + 128]

    w = -_softplus(-(w0 + _dg(jnp.tanh(z).astype(BF16), w2))) - 0.5
    logd = -jnp.exp(w)
    alr = _sigmoid(a0 + _dg(z.astype(BF16), a2))
    g = _dg(_sigmoid(z).astype(BF16), g2)
    lp = _dot_exact_lhs(tri, logd)
    yield

    kk = k * kkg
    kk = kk / jnp.maximum(jnp.sqrt(_dot1(kk * kk, seg)), 1e-12)
    k2 = k * (1.0 + (alr - 1.0) * ka)
    av = -kk
    bv = kk * alr
    bonus = _dot1(r * k2 * rk, seg) * v
    yield

    lp_c = lp[C - 1:C, :]
    e_neg = jnp.exp(-lp)
    e_rem = jnp.exp(lp_c - lp)
    ah = av * jnp.exp(lp - logd)
    rh = r * jnp.exp(lp)
    bh = bv * e_neg
    kh = k2 * e_neg
    b2 = bv * e_rem
    kc = k2 * e_rem
    p_c = jnp.exp(lp_c)

    lane_head = _iota((C, W), 1) >> 6
    hm = [lane_head == h for h in range(RW_HEADS)]
    zeros = jnp.zeros((C, W), F32)
    lhs = jnp.concatenate([jnp.where(m, ah, zeros) for m in hm] + [jnp.where(m, rh, zeros) for m in hm], axis=0)
    lhs = lhs.astype(BF16)
    xb = _dot1(lhs, bh, _NT).reshape(8, C, C)
    xk = _dot1(lhs, kh, _NT).reshape(8, C, C)
    yield

    ti = _iota((RW_HEADS, C, C), 1)
    tj = _iota((RW_HEADS, C, C), 2)
    strict = ti > tj
    incl = ti >= tj
    lm = jnp.where(strict, xb[0:4], 0.0)
    mak = jnp.where(strict, xk[0:4], 0.0)
    nrb = jnp.where(incl, xb[4:8], 0.0)
    nrk = jnp.where(incl, xk[4:8], 0.0)

    def bmm(a, b):
        return _dot1(a, b, _BNN)

    tinv = jnp.where(ti == tj, 1.0, 0.0) + lm
    lpow = lm
    for _ in range(5):
        lpow = bmm(lpow, lpow)
        yield
        tinv = tinv + bmm(tinv, lpow)
        yield

    def apply(m, x):
        full = _dot1(m.reshape(RW_HEADS * C, C), x).reshape(RW_HEADS, C, W)
        out = jnp.where(hm[0], full[0], zeros)
        for h in range(1, RW_HEADS):
            out = out + jnp.where(hm[h], full[h], zeros)
        return out

    tm = bmm(tinv, mak)
    a2h = apply(tinv, ah)
    nkv = apply(nrk, v)
    yield
    wv = apply(tm, v)
    r2 = rh + apply(nrb, a2h)
    bd = (_iota((W, W), 0) >> 6) == (_iota((W, W), 1) >> 6)
    eye = _iota((W, W), 0) == _iota((W, W), 1)
    a_c = jnp.where(eye, jnp.broadcast_to(p_c, (W, W)), 0.0) + jnp.where(bd, _dot1(b2, a2h, _TN), 0.0)
    yield
    y0 = apply(nrb, wv) + nkv
    g0 = jnp.where(bd, _dot1(b2, wv, _TN) + _dot1(kc, v, _TN), 0.0)
    s0 = st_ref[bi].astype(BF16)
    rs = _dot1(r2, s0)
    st_new = _dot1(a_c, s0)
    yield
    y = rs + y0
    st_ref[bi] = st_new + g0
    mean = _dot1(y, seg) * (1.0 / RW_N)
    yield
    yc = y - mean
    var = _dot1(yc * yc, seg) * (1.0 / RW_N)
    yield
    yn = yc * lax.rsqrt(var + RW_LN_EPS) * lng + lnb
    o_ref[bi] = ((yn + bonus) * g).astype(o_ref.dtype)


def _rw_kernel(p_ref, mu_ref, w0_ref, w2_ref, a0_ref, a2_ref, g2_ref, kk_ref, ka_ref, rk_ref,
               lng_ref, lnb_ref, seg_ref, tri_ref, o_ref, st_ref, prev_ref):
    @pl.when(pl.program_id(0) == 0)
    def _():
        st_ref[...] = jnp.zeros_like(st_ref)
        prev_ref[...] = jnp.zeros_like(prev_ref)

    prm = tuple(ref[...] for ref in (mu_ref, w0_ref, w2_ref, a0_ref, a2_ref, g2_ref, kk_ref, ka_ref, rk_ref,
                                     lng_ref, lnb_ref, seg_ref, tri_ref))
    _round_robin([_rw_chunk(p_ref[bi].astype(F32), st_ref, prev_ref, o_ref, bi, prm)
                  for bi in range(p_ref.shape[0])])


def _rw_mixer(p, prm, B, L):
    C = RW_CHUNK
    seg = (jnp.arange(256)[:, None] // RW_N == jnp.arange(256)[None, :] // RW_N).astype(BF16)
    tri = (jnp.arange(C)[None, :] <= jnp.arange(C)[:, None]).astype(BF16)
    fixed = lambda l: (0, 0)
    consts = tuple(prm) + (seg, tri)
    return pl.pallas_call(
        _rw_kernel,
        grid=(L // C,),
        in_specs=[pl.BlockSpec((B, C, RW_COLS), lambda l: (0, l, 0))]
        + [pl.BlockSpec(c.shape, fixed) for c in consts],
        out_specs=pl.BlockSpec((B, C, W_GROUP), lambda l: (0, l, 0)),
        out_shape=jax.ShapeDtypeStruct((B, L, W_GROUP), ACT),
        scratch_shapes=[pltpu.VMEM((B, W_GROUP, W_GROUP), F32), pltpu.VMEM((B, 8, RW_COLS), F32)],
        compiler_params=_cparams("arbitrary"),
        name="rwkv7_mixer",
    )(p.reshape(B, L, RW_COLS), *consts)


def _rw_params(mu, w0, w2, a0, a2, g2, k_k, k_a, r_k, ln_g, ln_b):
    row = lambda t: t.reshape(1, -1)
    w2p = jnp.zeros((128, W_GROUP), F32).at[0:RW_W_RANK].set(w2).astype(BF16)
    a2p = jnp.zeros((128, W_GROUP), F32).at[RW_W_RANK:RW_W_RANK + RW_A_RANK].set(a2).astype(BF16)
    g2p = jnp.zeros((128, W_GROUP), F32).at[RW_W_RANK + RW_A_RANK:].set(g2).astype(BF16)
    return (row(mu), row(w0), w2p, row(a0), a2p, g2p, row(k_k), row(k_a), row(r_k), row(ln_g), row(ln_b))


def _conv_kernel(p_ref, w_ref, b_ref, lg_ref, lb_ref, o_ref, u_ref, s_ref):
    tl = p_ref.shape[0]
    H = CONV_HALO
    RC = 64
    n = tl + H - SUBLANES

    @pl.when(pl.program_id(1) == 0)
    def _():
        u_ref[0:H, :] = jnp.zeros((H, W_GROUP), F32)

    u_ref[H:H + tl, :] = p_ref[:, 0:W_GROUP].astype(F32) * _sigmoid(p_ref[:, W_GROUP:2 * W_GROUP].astype(F32))
    for b in range(1, SUBLANES):
        for r0 in range(0, n, RC):
            rows = min(RC, n - r0)
            s_ref[b, r0:r0 + rows, :] = u_ref[r0 + b:r0 + b + rows, :]
    off = H - (CONV_WIDTH - 1)
    for c in range(tl // RC):
        acc = jnp.zeros((RC, W_GROUP), F32)
        for j in range(CONV_WIDTH):
            a, b = divmod(off + j, SUBLANES)
            lo = c * RC + a * SUBLANES
            tap = u_ref[lo:lo + RC, :] if b == 0 else s_ref[b, lo:lo + RC, :]
            acc = acc + w_ref[j:j + 1, :] * tap
        y = acc + b_ref[...]
        mu = jnp.mean(y, axis=-1, keepdims=True)
        yc = y - mu
        yn = yc * lax.rsqrt(jnp.mean(yc * yc, axis=-1, keepdims=True) + CONV_LN_EPS) * lg_ref[...] + lb_ref[...]
        o_ref[c * RC:(c + 1) * RC, :] = (yn * _sigmoid(yn)).astype(o_ref.dtype)
    tail = u_ref[tl:tl + H, :]
    u_ref[0:H, :] = tail


def _conv_mixer(p, prm, B, L, tl=512):
    w, b, lg, lb = prm
    fixed = lambda b_, l: (0, 0)
    return pl.pallas_call(
        _conv_kernel,
        grid=(B, L // tl),
        in_specs=[pl.BlockSpec((None, tl, 2 * W_GROUP), lambda b_, l: (b_, l, 0)),
                  pl.BlockSpec(w.shape, fixed), pl.BlockSpec(b.shape, fixed),
                  pl.BlockSpec(lg.shape, fixed), pl.BlockSpec(lb.shape, fixed)],
        out_specs=pl.BlockSpec((None, tl, W_GROUP), lambda b_, l: (b_, l, 0)),
        out_shape=jax.ShapeDtypeStruct((B, L, W_GROUP), ACT),
        scratch_shapes=[pltpu.VMEM((tl + CONV_HALO, W_GROUP), F32),
                        pltpu.VMEM((SUBLANES, tl + CONV_HALO - SUBLANES, W_GROUP), F32)],
        compiler_params=_cparams("parallel", "arbitrary"),
        name="conv_mixer",
    )(p.reshape(B, L, 2 * W_GROUP), w, b, lg, lb)


def _xattn_body(x, g_ref, wq_ref, k_ref, v_ref, wo_ref):
    q = _dg(_rms(x, g_ref[...]).astype(BF16), wq_ref[...]).astype(BF16)
    heads = [None] * X_HEADS

    def head(hd):
        lo = hd * X_HEAD_DIM
        s = _dg(q[:, lo:lo + X_HEAD_DIM], k_ref[:, lo:lo + X_HEAD_DIM], _NT) * (X_HEAD_DIM ** -0.5)
        yield
        e = jnp.exp(s - jnp.max(s, axis=-1, keepdims=True))
        pr = e / jnp.sum(e, axis=-1, keepdims=True)
        heads[hd] = _dg(pr.astype(BF16), v_ref[:, lo:lo + X_HEAD_DIM])

    _round_robin([head(hd) for hd in range(X_HEADS)])
    o = jnp.concatenate(heads, axis=1).astype(BF16)
    return x + _dg(o, wo_ref[...])


def _router_body(h, g_ref, wr_ref, br_ref, xn_ref, eid_ref, gate_ref):
    xn = _rms(h, g_ref[...])
    xn_ref[...] = _pack_rows(xn)
    lg = _dot1(wr_ref[...], xn, _NT) + br_ref[...]
    tm = lg.shape[1]
    gl = lg[0:8, :]
    sub = _iota((8, tm), 0)
    gmax = jnp.max(gl, axis=0, keepdims=True)
    gsel = jnp.min(jnp.where(gl == gmax, sub, 8), axis=0, keepdims=True)
    g_w = 1.0 / jnp.sum(jnp.exp(gl - gmax), axis=0, keepdims=True)
    e_in = jnp.zeros((8, tm), F32)
    for grp in range(N_GROUPS):
        e_in = e_in + jnp.where(gsel == grp, lg[8 + 8 * grp:16 + 8 * grp, :], 0.0)
    t1 = jnp.max(e_in, axis=0, keepdims=True)
    i1 = jnp.min(jnp.where(e_in == t1, sub, 8), axis=0, keepdims=True)
    rest = jnp.where(sub == i1, -jnp.inf, e_in)
    t2 = jnp.max(rest, axis=0, keepdims=True)
    i2 = jnp.min(jnp.where(rest == t2, sub, 8), axis=0, keepdims=True)
    e2 = jnp.exp(t2 - t1)
    p1 = 1.0 / (1.0 + e2)
    eid_ref[0:1, :] = gsel * EXP_PER_GROUP + i1
    eid_ref[1:2, :] = gsel * EXP_PER_GROUP + i2
    gate_ref[0:1, :] = p1 * g_w
    gate_ref[1:2, :] = (e2 * p1) * g_w


def _post_mix_kernel(y5_ref, yg_ref, yr_ref, yc_ref, beta_ref, wout_ref, h_ref, gx_ref, wq_ref, k_ref, v_ref,
                     wo_ref, gf_ref, wr_ref, br_ref, h2_ref, xn_ref, eid_ref, gate_ref):
    h1 = _out_proj_body((y5_ref, yg_ref, yr_ref, yc_ref), beta_ref, wout_ref, h_ref[...])
    h2 = _xattn_body(h1, gx_ref, wq_ref, k_ref, v_ref, wo_ref)
    h2_ref[...] = h2
    _router_body(h2, gf_ref, wr_ref, br_ref, xn_ref, eid_ref, gate_ref)


def _post_mix(ys, beta, w_out, h, gx, wq, k, v, wo, gf, wr, br, B, L, n_mem, tm=ROUTE_TILE):
    T = B * L
    nl = L // tm
    tile = lambda b, l: (b * nl + l, 0)
    tile3 = lambda b, l: (b * nl + l, 0, 0)
    fixed = lambda b, l: (0, 0)
    mem = lambda b, l: (b, 0, 0)
    mat = pl.BlockSpec((D_MODEL, D_MODEL), fixed)
    vec = pl.BlockSpec((1, D_MODEL), fixed)
    return pl.pallas_call(
        _post_mix_kernel,
        grid=(B, nl),
        in_specs=[pl.BlockSpec((tm, W_GROUP), tile)] * 4
        + [vec, mat, pl.BlockSpec((tm, D_MODEL), tile), vec, mat,
           pl.BlockSpec((None, n_mem, D_MODEL), mem), pl.BlockSpec((None, n_mem, D_MODEL), mem), mat, vec,
           pl.BlockSpec(wr.shape, fixed), pl.BlockSpec(br.shape, fixed)],
        out_specs=[pl.BlockSpec((tm, D_MODEL), tile), pl.BlockSpec((tm, D_PACK), tile),
                   pl.BlockSpec((None, 2, tm), tile3), pl.BlockSpec((None, 2, tm), tile3)],
        out_shape=[jax.ShapeDtypeStruct((T, D_MODEL), F32), jax.ShapeDtypeStruct((T, D_PACK), U32),
                   jax.ShapeDtypeStruct((T // tm, 2, tm), jnp.int32),
                   jax.ShapeDtypeStruct((T // tm, 2, tm), F32)],
        compiler_params=_cparams("parallel", "parallel"),
        name="post_mix",
    )(*ys, beta, w_out, h, gx, wq, k.reshape(B, n_mem, D_MODEL), v.reshape(B, n_mem, D_MODEL), wo, gf, wr, br)


def _rank_kernel(eid_ref, ut_ref, rank_ref, cnt_ref, carry_ref):
    tm = eid_ref.shape[1]

    @pl.when(pl.program_id(0) == 0)
    def _():
        carry_ref[...] = jnp.zeros_like(carry_ref)

    ex = _iota((N_EXPERTS, tm), 0)
    oh0 = jnp.where(ex == eid_ref[0:1, :], 1.0, 0.0)
    oh1 = jnp.where(ex == eid_ref[1:2, :], 1.0, 0.0)
    oh = oh0 + oh1
    cum = _dg(oh.astype(BF16), ut_ref[...])
    carry = carry_ref[:, 0:1]
    before = cum - oh + carry
    rank_ref[0:1, :] = jnp.sum(oh0 * before, axis=0, keepdims=True).astype(jnp.int32)
    rank_ref[1:2, :] = jnp.sum(oh1 * before, axis=0, keepdims=True).astype(jnp.int32)
    total = carry + cum[:, tm - 1:tm]
    carry_ref[...] = jnp.broadcast_to(total, carry_ref.shape)
    cnt_ref[...] = jnp.broadcast_to(total, cnt_ref.shape).astype(jnp.int32)


def _rank(eid, tm=ROUTE_TILE):
    nt = eid.shape[0]
    ut = (jnp.arange(tm)[:, None] <= jnp.arange(tm)[None, :]).astype(BF16)
    return pl.pallas_call(
        _rank_kernel,
        grid=(nt,),
        in_specs=[pl.BlockSpec((None, 2, tm), lambda i: (i, 0, 0)), pl.BlockSpec((tm, tm), lambda i: (0, 0))],
        out_specs=[pl.BlockSpec((None, 2, tm), lambda i: (i, 0, 0)),
                   pl.BlockSpec((N_EXPERTS, 128), lambda i: (0, 0))],
        out_shape=[jax.ShapeDtypeStruct((nt, 2, tm), jnp.int32),
                   jax.ShapeDtypeStruct((N_EXPERTS, 128), jnp.int32)],
        scratch_shapes=[pltpu.VMEM((N_EXPERTS, 128), F32)],
        compiler_params=_cparams("arbitrary"),
        name="moe_rank",
    )(eid, ut)


def _slot_kernel(ps_ref, eid_ref, rank_ref, slot_ref):
    eid = eid_ref[...]
    acc = rank_ref[...]
    for e in range(N_EXPERTS):
        acc = acc + jnp.where(eid == e, ps_ref[e], 0)
    slot_ref[...] = acc


def _slots(pad_starts, eid, rank):
    nt, _, tm = eid.shape
    shape2d = (nt * 2, tm)
    out = pl.pallas_call(
        _slot_kernel,
        grid_spec=pltpu.PrefetchScalarGridSpec(
            num_scalar_prefetch=1,
            grid=(1,),
            in_specs=[pl.BlockSpec(shape2d, lambda i, ps: (0, 0))] * 2,
            out_specs=pl.BlockSpec(shape2d, lambda i, ps: (0, 0)),
        ),
        out_shape=jax.ShapeDtypeStruct(shape2d, jnp.int32),
        compiler_params=_cparams("arbitrary"),
        name="moe_slots",
    )(pad_starts, eid.reshape(shape2d), rank.reshape(shape2d))
    return out.reshape(nt, 2, tm).transpose(0, 2, 1).reshape(nt * tm * 2)


def _dispatch_kernel(lo_ref, np_ref, nu_ref, slot_ref, x_ref, xs_ref, z_ref, sem):
    groups = x_ref.shape[0]
    n_blocks = xs_ref.shape[0] // MOE_BLOCK

    def zero_block(blk):
        rows = pl.ds(pl.multiple_of(blk * MOE_BLOCK, MOE_BLOCK), MOE_BLOCK)
        return pltpu.make_async_copy(z_ref, xs_ref.at[rows], sem)

    def zero_fill(act):
        def per_expert(e, c):
            @pl.when(np_ref[e] > 0)
            def _():
                act(zero_block(lo_ref[e] // MOE_BLOCK))
            return c

        def tail(blk, c):
            act(zero_block(blk))
            return c

        lax.fori_loop(0, N_EXPERTS, per_expert, 0)
        lax.fori_loop(nu_ref[0], n_blocks, tail, 0)

    @pl.when(pl.program_id(0) == 0)
    def _():
        z_ref[...] = jnp.zeros_like(z_ref)
        zero_fill(lambda cp: cp.start())
        zero_fill(lambda cp: cp.wait())

    def issue(i, _):
        for u in range(SUBLANES):
            for k in range(2):
                s = slot_ref[i * (2 * SUBLANES) + (2 * u + k)]
                pltpu.make_async_copy(x_ref.at[i, pl.ds(u, 1)], xs_ref.at[pl.ds(s, 1)], sem).start(priority=k)
        return 0

    lax.fori_loop(0, groups, issue, 0)
    for _ in range(2):
        pltpu.make_async_copy(x_ref, x_ref, sem).wait()


def _dispatch(pad_lo, n_pad, n_used, slot, xn, n_slots, tm=ROUTE_TILE):
    T = xn.shape[0]
    g = tm // SUBLANES
    return pl.pallas_call(
        _dispatch_kernel,
        grid_spec=pltpu.PrefetchScalarGridSpec(
            num_scalar_prefetch=3,
            grid=(T // tm,),
            in_specs=[pl.BlockSpec((2 * tm,), lambda i, lo, npad, nu: (i,), memory_space=pltpu.SMEM),
                      pl.BlockSpec((g, SUBLANES, D_PACK), lambda i, lo, npad, nu: (i, 0, 0))],
            out_specs=pl.BlockSpec(memory_space=pl.ANY),
            scratch_shapes=[pltpu.VMEM((MOE_BLOCK, D_PACK), U32), pltpu.SemaphoreType.DMA(())],
        ),
        out_shape=jax.ShapeDtypeStruct((n_slots, D_PACK), U32),
        compiler_params=_cparams("arbitrary"),
        name="moe_dispatch",
    )(pad_lo, n_pad, n_used, slot, xn.reshape(T // SUBLANES, SUBLANES, D_PACK))


def _expert_kernel(be_ref, nu_ref, x_ref, wg_ref, wu_ref, wd_ref, y_ref, wgb_ref, wub_ref, wdb_ref):
    i = pl.program_id(0)
    used = i < nu_ref[0]

    @pl.when(jnp.logical_and(used, jnp.logical_or(i == 0, be_ref[i] != be_ref[jnp.maximum(i - 1, 0)])))
    def _():
        wgb_ref[...] = wg_ref[...].astype(BF16)
        wub_ref[...] = wu_ref[...].astype(BF16)
        wdb_ref[...] = wd_ref[...].astype(BF16)

    @pl.when(used)
    def _():
        xb = _unpack_rows(x_ref[...]).astype(BF16)
        gate = _dg(xb, wgb_ref[...])
        hid = (gate * _sigmoid(gate)) * _dg(xb, wub_ref[...])
        y_ref[...] = _pack_rows(_dg(hid.astype(BF16), wdb_ref[...]))

    @pl.when(jnp.logical_not(used))
    def _():
        y_ref[...] = jnp.zeros_like(y_ref)


def _experts(block_exp, n_used, xs, w_gate, w_up, w_down, layer):
    n_slots = xs.shape[0]
    nb = n_slots // MOE_BLOCK
    last = lambda i, nu: jnp.minimum(i, nu[0] - 1)
    wmap = lambda i, be, nu: (layer, be[last(i, nu)], 0, 0)
    return pl.pallas_call(
        _expert_kernel,
        grid_spec=pltpu.PrefetchScalarGridSpec(
            num_scalar_prefetch=2,
            grid=(nb,),
            in_specs=[pl.BlockSpec((MOE_BLOCK, D_PACK), lambda i, be, nu: (last(i, nu), 0)),
                      pl.BlockSpec((None, None, D_MODEL, D_EXPERT), wmap),
                      pl.BlockSpec((None, None, D_MODEL, D_EXPERT), wmap),
                      pl.BlockSpec((None, None, D_EXPERT, D_MODEL), wmap)],
            out_specs=pl.BlockSpec((MOE_BLOCK, D_PACK), lambda i, be, nu: (i, 0)),
            scratch_shapes=[pltpu.VMEM((D_MODEL, D_EXPERT), BF16), pltpu.VMEM((D_MODEL, D_EXPERT), BF16),
                            pltpu.VMEM((D_EXPERT, D_MODEL), BF16)],
        ),
        out_shape=jax.ShapeDtypeStruct((n_slots, D_PACK), U32),
        compiler_params=_cparams("arbitrary"),
        name="moe_experts",
    )(block_exp, n_used, xs, w_gate, w_up, w_down)


def _combine_kernel(slot_ref, gate_ref, h_ref, ys_ref, o_ref, y0_ref, y1_ref, sem):
    tm = h_ref.shape[0]
    groups = tm // SUBLANES
    bufs = (y0_ref, y1_ref)

    def issue(i, _):
        for u in range(SUBLANES):
            for k in range(2):
                s = slot_ref[i * (2 * SUBLANES) + (2 * u + k)]
                pltpu.make_async_copy(ys_ref.at[pl.ds(s, 1)], bufs[k].at[i, pl.ds(u, 1)], sem).start(priority=k)
        return 0

    lax.fori_loop(0, groups, issue, 0)
    for k in range(2):
        pltpu.make_async_copy(bufs[k], bufs[k], sem).wait()
    y0 = _unpack_rows(y0_ref[...].reshape(tm, D_PACK))
    y1 = _unpack_rows(y1_ref[...].reshape(tm, D_PACK))
    o_ref[...] = h_ref[...] + gate_ref[:, 0:1] * y0 + gate_ref[:, 1:2] * y1


def _combine(slot, gate_t, h, ys, tm=ROUTE_TILE):
    T = h.shape[0]
    g = tm // SUBLANES
    return pl.pallas_call(
        _combine_kernel,
        grid=(T // tm,),
        in_specs=[pl.BlockSpec((2 * tm,), lambda i: (i,), memory_space=pltpu.SMEM),
                  pl.BlockSpec((tm, 2), lambda i: (i, 0)),
                  pl.BlockSpec((tm, D_MODEL), lambda i: (i, 0)),
                  pl.BlockSpec(memory_space=pl.ANY)],
        out_specs=pl.BlockSpec((tm, D_MODEL), lambda i: (i, 0)),
        scratch_shapes=[pltpu.VMEM((g, SUBLANES, D_PACK), U32), pltpu.VMEM((g, SUBLANES, D_PACK), U32),
                        pltpu.SemaphoreType.DMA(())],
        out_shape=jax.ShapeDtypeStruct((T, D_MODEL), F32),
        compiler_params=_cparams("arbitrary"),
        name="moe_combine",
    )(slot, gate_t, h, ys)


def _router_params(group_w, group_b, expert_w, expert_b):
    wr = jnp.zeros((ROUTE_ROWS, D_MODEL), F32).at[0:N_GROUPS].set(group_w.T).at[8:].set(expert_w.T)
    br = jnp.full((ROUTE_ROWS, 1), -1e30, F32).at[0:N_GROUPS, 0].set(group_b).at[8:, 0].set(expert_b)
    return wr, br


def _moe(h, xn, eid, gate, w_gate, w_up, w_down, layer):
    T = h.shape[0]
    rank, cnt = _rank(eid)
    counts = cnt[:, 0]
    padded = (counts + MOE_BLOCK - 1) // MOE_BLOCK * MOE_BLOCK
    pad_ends = jnp.cumsum(padded)
    pad_starts = (pad_ends - padded).astype(jnp.int32)
    n_slots = ((T * 2 + MOE_BLOCK - 1) // MOE_BLOCK + N_EXPERTS) * MOE_BLOCK
    block_start = jnp.arange(n_slots // MOE_BLOCK, dtype=jnp.int32) * MOE_BLOCK
    owner = jnp.sum((pad_ends[None, :] <= block_start[:, None]).astype(jnp.int32), axis=1)
    block_exp = jnp.minimum(owner, N_EXPERTS - 1).astype(jnp.int32)
    slot = _slots(pad_starts, eid, rank)
    n_used = (pad_ends[N_EXPERTS - 1:] // MOE_BLOCK).astype(jnp.int32)
    xs = _dispatch((pad_starts + counts).astype(jnp.int32), (padded - counts).astype(jnp.int32), n_used,
                   slot, xn, n_slots)
    ys = _experts(block_exp, n_used, xs, w_gate, w_up, w_down, layer)
    gate_t = gate.transpose(0, 2, 1).reshape(T, 2)
    return _combine(slot, gate_t, h, ys)


def kernel(x, mem, norm_mix_g, w_in, w_out, mix_beta, s5_lam_re, s5_lam_im, s5_b_re, s5_b_im, s5_c_re, s5_c_im, s5_d, s5_log_dt, s5_glu_w, s5_glu_b, gla_w_up, gla_b_up, gla_norm_g, rw_mu, rw_w0, rw_w2, rw_a0, rw_a2, rw_g2, rw_k_k, rw_k_a, rw_r_k, rw_ln_g, rw_ln_b, conv_w, conv_b, conv_ln_g, conv_ln_b, norm_xattn_g, norm_mem_g, xa_wq, xa_wk, xa_wv, xa_wo, norm_ffn_g, moe_group_w, moe_group_b, moe_expert_w, moe_expert_b, moe_w_gate, moe_w_up, moe_w_down, norm_final_g):
    B, L, D = x.shape
    n_mem = mem.shape[1]
    depth = w_in.shape[0]
    T = B * L
    h = x.reshape(T, D)
    mem2d = mem.reshape(B * n_mem, D)
    row = lambda t: t.reshape(1, -1)
    c0 = W_GROUP
    c1 = c0 + 784
    c2 = c1 + RW_COLS
    for l in range(depth):
        wl = w_in[l]
        w5 = wl[:, :c0].astype(BF16)
        wg = jnp.pad(wl[:, c0:c1], ((0, 0), (0, GLA_COLS_PAD - 784))).astype(BF16)
        wr = wl[:, c1:c2].astype(BF16)
        wc = wl[:, c2:].astype(BF16)
        p5, pg, pr, pc = _in_proj(h, row(norm_mix_g[l]), w5, wg, wr, wc)
        y5 = _s5_mixer(p5, _s5_params(s5_lam_re[l], s5_lam_im[l], s5_b_re[l], s5_b_im[l], s5_c_re[l], s5_c_im[l],
                                      s5_d[l], s5_log_dt[l], s5_glu_w[l], s5_glu_b[l]), B, L)
        yg = _gla_mixer(pg, _gla_params(gla_w_up[l], gla_b_up[l], gla_norm_g[l]), B, L)
        yr = _rw_mixer(pr, _rw_params(rw_mu[l], rw_w0[l], rw_w2[l], rw_a0[l], rw_a2[l], rw_g2[l], rw_k_k[l],
                                      rw_k_a[l], rw_r_k[l], rw_ln_g[l], rw_ln_b[l]), B, L)
        yc = _conv_mixer(pc, (conv_w[l], row(conv_b[l]), row(conv_ln_g[l]), row(conv_ln_b[l])), B, L)
        ys = [t.reshape(T, W_GROUP) for t in (y5, yg, yr, yc)]
        kmem, vmem = _kv_proj(mem2d, row(norm_mem_g[l]), xa_wk[l].astype(BF16), xa_wv[l].astype(BF16))
        wr, br = _router_params(moe_group_w[l], moe_group_b[l], moe_expert_w[l], moe_expert_b[l])
        h, xn, eid, gate = _post_mix(ys, row(mix_beta[l]), w_out[l].astype(BF16), h, row(norm_xattn_g[l]),
                                     xa_wq[l].astype(BF16), kmem, vmem, xa_wo[l].astype(BF16),
                                     row(norm_ffn_g[l]), wr, br, B, L, n_mem)
        h = _moe(h, xn, eid, gate, moe_w_gate, moe_w_up, moe_w_down, l)
    return _final_norm(h, row(norm_final_g)).reshape(B, L, D)
```

```python
import math

import jax
import jax.numpy as jnp
from jax import lax
from jax.experimental import pallas as pl
from jax.experimental.pallas import tpu as pltpu

F32 = jnp.float32
BF16 = jnp.bfloat16
ACT = jnp.bfloat16

D_MODEL = 1024
W_GROUP = 256
EPS = 1e-6

S5_GROUP_CH = 16
S5_GROUPS = 16
S5_STATE = 64
S5_NS = S5_GROUPS * S5_STATE

GLA_HEADS = 4
GLA_DV = 64
GLA_DK = 32
GLA_RANK = 16
GLA_TAU = 16.0
GLA_CHUNK = 16
GLA_COLS_PAD = 896

RW_HEADS = 4
RW_N = 64
RW_W_RANK = 32
RW_A_RANK = 32
RW_G_RANK = 64
RW_LN_EPS = 64e-5
RW_COLS = 896
RW_CHUNK = 64

CONV_WIDTH = 31
CONV_LN_EPS = 1e-5
CONV_HALO = 32

X_HEADS = 4
X_HEAD_DIM = 256

N_GROUPS = 4
EXP_PER_GROUP = 8
N_EXPERTS = 32
D_EXPERT = 512
MOE_BLOCK = 256
ROUTE_ROWS = 40
ROUTE_TILE = 512

SUBLANES = 8
VMEM_LIMIT = 56 * 1024 * 1024


def _cparams(*sem):
    return pltpu.CompilerParams(dimension_semantics=sem, vmem_limit_bytes=VMEM_LIMIT)


_NN = (((1,), (0,)), ((), ()))
_NT = (((1,), (1,)), ((), ()))
_TN = (((0,), (0,)), ((), ()))
_BNN = (((2,), (1,)), ((0,), (0,)))


def _dg(a, b, dims=_NN):
    return lax.dot_general(a, b, dims, preferred_element_type=F32)


def _split3(a):
    hi = a.astype(BF16)
    r1 = a - hi.astype(F32)
    mid = r1.astype(BF16)
    lo = (r1 - mid.astype(F32)).astype(BF16)
    return hi, mid, lo


def _dot1(a, b, dims=_NN):
    return _dg(a.astype(BF16), b.astype(BF16), dims)


def _dot_exact_lhs(a_bf16, b, dims=_NN):
    bh, bm, bl = _split3(b)
    return _dg(a_bf16, bh, dims) + (_dg(a_bf16, bm, dims) + _dg(a_bf16, bl, dims))


def _iota(shape, dim):
    return lax.broadcasted_iota(jnp.int32, shape, dim)


def _rms(x, g):
    return x * lax.rsqrt(jnp.mean(x * x, axis=-1, keepdims=True) + EPS) * g


def _sigmoid(x):
    return 1.0 / (1.0 + jnp.exp(-x))


def _softplus(x):
    return jnp.maximum(x, 0.0) + jnp.log1p(jnp.exp(-jnp.abs(x)))


D_PACK = D_MODEL // 2
U32 = jnp.uint32


def _pack_rows(x):
    bits = lax.bitcast_convert_type(x.astype(BF16).astype(F32), U32)
    return (bits[:, :D_PACK] >> 16) | bits[:, D_PACK:]


def _unpack_rows(w):
    lo = lax.bitcast_convert_type(w << 16, F32)
    hi = lax.bitcast_convert_type(w & jnp.uint32(0xFFFF0000), F32)
    return jnp.concatenate([lo, hi], axis=1)


def _round_robin(chains):
    done = object()
    while chains:
        chains = [c for c in chains if next(c, done) is not done]


def _shift_rows(x, s, fill=None):
    rolled = pltpu.roll(x, s, 0)
    rows = _iota(x.shape, 0)
    if fill is None:
        fill = jnp.zeros_like(x)
    return jnp.where(rows >= s, rolled, fill)


def _in_proj_kernel(x_ref, g_ref, w5_ref, wg_ref, wr_ref, wc_ref, o5_ref, og_ref, or_ref, oc_ref):
    xb = _rms(x_ref[...], g_ref[...]).astype(BF16)
    o5_ref[...] = _dg(xb, w5_ref[...]).astype(o5_ref.dtype)
    og_ref[...] = _dg(xb, wg_ref[...]).astype(og_ref.dtype)
    or_ref[...] = _dg(xb, wr_ref[...]).astype(or_ref.dtype)
    oc_ref[...] = _dg(xb, wc_ref[...]).astype(oc_ref.dtype)


def _in_proj(h, g, w5, wg, wr, wc, tm=1024):
    T = h.shape[0]
    ws = (w5, wg, wr, wc)
    row = lambda i: (i, 0)
    fixed = lambda i: (0, 0)
    return pl.pallas_call(
        _in_proj_kernel,
        grid=(T // tm,),
        in_specs=[pl.BlockSpec((tm, D_MODEL), row), pl.BlockSpec((1, D_MODEL), fixed)]
        + [pl.BlockSpec(w.shape, fixed) for w in ws],
        out_specs=[pl.BlockSpec((tm, w.shape[1]), row) for w in ws],
        out_shape=[jax.ShapeDtypeStruct((T, w.shape[1]), ACT) for w in ws],
        compiler_params=_cparams("parallel"),
        name="in_proj",
    )(h, g, *ws)


def _kv_proj_kernel(m_ref, g_ref, wk_ref, wv_ref, k_ref, v_ref):
    mb = _rms(m_ref[...], g_ref[...]).astype(BF16)
    k_ref[...] = _dg(mb, wk_ref[...]).astype(BF16)
    v_ref[...] = _dg(mb, wv_ref[...]).astype(BF16)


def _kv_proj(mem2d, g, wk, wv, tm=256):
    R = mem2d.shape[0]
    row = lambda i: (i, 0)
    fixed = lambda i: (0, 0)
    return pl.pallas_call(
        _kv_proj_kernel,
        grid=(R // tm,),
        in_specs=[pl.BlockSpec((tm, D_MODEL), row), pl.BlockSpec((1, D_MODEL), fixed),
                  pl.BlockSpec((D_MODEL, D_MODEL), fixed), pl.BlockSpec((D_MODEL, D_MODEL), fixed)],
        out_specs=[pl.BlockSpec((tm, D_MODEL), row)] * 2,
        out_shape=[jax.ShapeDtypeStruct((R, D_MODEL), BF16)] * 2,
        compiler_params=_cparams("parallel"),
        name="kv_proj",
    )(mem2d, g, wk, wv)


def _out_proj_body(y_refs, beta_ref, w_ref, h):
    ys = [y_ref[...].astype(F32) for y_ref in y_refs]
    mixed = jnp.concatenate(ys, axis=1) * beta_ref[...]
    return h + _dg(mixed.astype(BF16), w_ref[...])


def _final_norm_kernel(x_ref, g_ref, o_ref):
    o_ref[...] = _rms(x_ref[...], g_ref[...])


def _final_norm(h, g, tm=1024):
    T = h.shape[0]
    return pl.pallas_call(
        _final_norm_kernel,
        grid=(T // tm,),
        in_specs=[pl.BlockSpec((tm, D_MODEL), lambda i: (i, 0)), pl.BlockSpec((1, D_MODEL), lambda i: (0, 0))],
        out_specs=pl.BlockSpec((tm, D_MODEL), lambda i: (i, 0)),
        out_shape=jax.ShapeDtypeStruct((T, D_MODEL), F32),
        compiler_params=_cparams("parallel"),
        name="final_norm",
    )(h, g)


S5_BLOCKS = S5_NS // 128


def _s5_kernel(u_ref, bpad_ref, cpad_ref, lam_ref, d_ref, gw_ref, gb_ref, o_ref, xr_ref, xi_ref, carry_ref):
    nb, tl = u_ref.shape[0], u_ref.shape[1]
    nblk = S5_BLOCKS

    @pl.when(pl.program_id(0) == 0)
    def _():
        carry_ref[...] = jnp.zeros_like(carry_ref)

    for bi in range(nb):
        ub = u_ref[bi].astype(BF16)
        for s in range(nblk):
            bu = _dg(ub, bpad_ref[s])
            xr_ref[bi, pl.ds(s, tl, stride=nblk), :] = bu[:, 0:128]
            xi_ref[bi, pl.ds(s, tl, stride=nblk), :] = bu[:, 128:256]

    lam_re = lam_ref[0]
    lam_im = lam_ref[1]

    def step(t, carry):
        r0 = pl.multiple_of(t * nblk, nblk)
        out = []
        for bi in range(nb):
            c_re, c_im = carry[2 * bi], carry[2 * bi + 1]
            x_re = (lam_re * c_re - lam_im * c_im) + xr_ref[bi, pl.ds(r0, nblk), :]
            x_im = (lam_re * c_im + lam_im * c_re) + xi_ref[bi, pl.ds(r0, nblk), :]
            xr_ref[bi, pl.ds(r0, nblk), :] = x_re
            xi_ref[bi, pl.ds(r0, nblk), :] = x_im
            out += [x_re, x_im]
        return tuple(out)

    init = tuple(carry_ref[i] for i in range(2 * nb))
    last = lax.fori_loop(0, tl, step, init, unroll=8)
    for i in range(2 * nb):
        carry_ref[i] = last[i]

    for bi in range(nb):
        y = d_ref[...] * u_ref[bi].astype(F32)
        for s in range(nblk):
            st = jnp.concatenate([xr_ref[bi, pl.ds(s, tl, stride=nblk), :],
                                  xi_ref[bi, pl.ds(s, tl, stride=nblk), :]], axis=1)
            y = y + _dg(st.astype(BF16), cpad_ref[s])
        y = 0.5 * y * (1.0 + jnp.tanh(math.sqrt(2.0 / math.pi) * (y + 0.044715 * (y * y * y))))
        gate = _dg(y.astype(BF16), gw_ref[...]) + gb_ref[...]
        o_ref[bi] = (y * _sigmoid(gate)).astype(o_ref.dtype)


def _s5_mixer(p, prm, B, L, tl=256):
    rows = tl * S5_BLOCKS
    return pl.pallas_call(
        _s5_kernel,
        grid=(L // tl,),
        in_specs=[pl.BlockSpec((B, tl, W_GROUP), lambda l: (0, l, 0))]
        + [pl.BlockSpec(c.shape, lambda l, n=c.ndim: (0,) * n) for c in prm],
        out_specs=pl.BlockSpec((B, tl, W_GROUP), lambda l: (0, l, 0)),
        out_shape=jax.ShapeDtypeStruct((B, L, W_GROUP), ACT),
        scratch_shapes=[pltpu.VMEM((B, rows, 128), F32), pltpu.VMEM((B, rows, 128), F32),
                        pltpu.VMEM((2 * B, S5_BLOCKS, 128), F32)],
        compiler_params=_cparams("arbitrary"),
        name="s5_mixer",
    )(p.reshape(B, L, W_GROUP), *prm)


def _s5_params(lam_re, lam_im, b_re, b_im, c_re, c_im, d_skip, log_dt, glu_w, glu_b):
    G, N, P = S5_GROUPS, S5_STATE, S5_GROUP_CH
    dt = jnp.exp(log_dt)[:, None]
    mag = jnp.exp(lam_re * dt)
    lb_re = mag * jnp.cos(lam_im * dt)
    lb_im = mag * jnp.sin(lam_im * dt)
    den = lam_re * lam_re + lam_im * lam_im
    n_re, n_im = lb_re - 1.0, lb_im
    f_re = (n_re * lam_re + n_im * lam_im) / den
    f_im = (n_im * lam_re - n_re * lam_im) / den
    bb_re = f_re[:, :, None] * b_re - f_im[:, :, None] * b_im
    bb_im = f_re[:, :, None] * b_im + f_im[:, :, None] * b_re
    eye = jnp.eye(G, dtype=F32)
    bbd_re = jnp.einsum('gnp,gh->gphn', bb_re, eye).reshape(G * P, G * N)
    bbd_im = jnp.einsum('gnp,gh->gphn', bb_im, eye).reshape(G * P, G * N)
    cbd_re = jnp.einsum('gpn,gh->gnhp', c_re, eye).reshape(G * N, G * P)
    cbd_im = jnp.einsum('gpn,gh->gnhp', -c_im, eye).reshape(G * N, G * P)
    nblk = S5_BLOCKS
    bpad = jnp.concatenate([bbd_re.reshape(G * P, nblk, 128).transpose(1, 0, 2),
                            bbd_im.reshape(G * P, nblk, 128).transpose(1, 0, 2)], axis=2).astype(BF16)
    cpad = jnp.concatenate([cbd_re.reshape(nblk, 128, G * P), cbd_im.reshape(nblk, 128, G * P)],
                           axis=1).astype(BF16)
    lam_tab = jnp.stack([lb_re.reshape(nblk, 128), lb_im.reshape(nblk, 128)])
    return (bpad, cpad, lam_tab, d_skip.reshape(1, W_GROUP), glu_w.astype(BF16), glu_b.reshape(1, W_GROUP))


def _gla_kernel(p_ref, wup_ref, bup_ref, tri_ref, sege_ref, segv_ref, ng_ref, o_ref,
                st_ref, b_ref, oacc_ref):
    C = GLA_CHUNK
    nb, tl = p_ref.shape[0], p_ref.shape[1]

    @pl.when(pl.program_id(0) == 0)
    def _():
        st_ref[...] = jnp.zeros_like(st_ref)

    for bi in range(nb):
        x = _dg(p_ref[bi, :, 768:896].astype(BF16), wup_ref[...]) + bup_ref[...]
        log_alpha = (jnp.minimum(x, 0.0) - jnp.log1p(jnp.exp(-jnp.abs(x)))) * (1.0 / GLA_TAU)
        b_ref[bi] = _dot_exact_lhs(tri_ref[...], log_alpha)

    row_i = _iota((C, 128), 0)
    bd_mask = (_iota((256, 128), 0) >> 6) == (_iota((256, 128), 1) >> 5)
    sege = sege_ref[...]

    def chain(bi, r0):
        q = p_ref[bi, pl.ds(r0, C), 0:128].astype(F32) * (GLA_DK ** -0.5)
        k = p_ref[bi, pl.ds(r0, C), 128:256].astype(F32)
        v = p_ref[bi, pl.ds(r0, C), 256:512].astype(F32)
        b = b_ref[bi, pl.ds(r0, C), :]
        parts = []
        for j in range(C):
            e = jnp.exp(b - b[j:j + 1, :])
            parts.append(jnp.where(row_i >= j, (q * k[j:j + 1, :]) * e, 0.0).astype(BF16))
        t = jnp.concatenate(parts, axis=0)
        w = _dg(t, sege)
        st = st_ref[bi]
        qe = (q * jnp.exp(b)).astype(BF16)
        o_inter = _dg(qe, st.astype(BF16), _NT)
        b_last = b[C - 1:C, :]
        kt = (k * jnp.exp(b_last - b)).astype(BF16)
        upd = _dg(v.astype(BF16), kt, _TN)
        yield
        o = o_inter + w[0:C, :] * v[0:1, :]
        for j in range(1, C):
            o = o + w[j * C:(j + 1) * C, :] * v[j:j + 1, :]
        st_ref[bi] = st * jnp.exp(b_last) + jnp.where(bd_mask, upd, 0.0)
        oacc_ref[bi, pl.ds(r0, C), :] = o

    def chunk(c, _):
        r0 = pl.multiple_of(c * C, C)
        _round_robin([chain(bi, r0) for bi in range(nb)])
        return 0

    lax.fori_loop(0, tl // C, chunk, 0)

    for bi in range(nb):
        o = oacc_ref[bi]
        ms = _dot1(o * o, segv_ref[...]) * (1.0 / GLA_DV)
        g = p_ref[bi, :, 512:768].astype(F32)
        o_ref[bi] = (o * lax.rsqrt(ms + EPS) * ng_ref[...] * (g * _sigmoid(g))).astype(o_ref.dtype)


def _gla_mixer(p, prm, B, L, tl=256):
    wup, bup, ng = prm
    C = GLA_CHUNK
    r = jnp.arange(tl)
    tri = ((r[:, None] // C == r[None, :] // C) & (r[None, :] <= r[:, None])).astype(BF16)
    sege = (jnp.arange(128)[:, None] // GLA_DK == jnp.arange(256)[None, :] // GLA_DV).astype(BF16)
    segv = (jnp.arange(256)[:, None] // GLA_DV == jnp.arange(256)[None, :] // GLA_DV).astype(BF16)
    fixed = lambda l: (0, 0)
    return pl.pallas_call(
        _gla_kernel,
        grid=(L // tl,),
        in_specs=[pl.BlockSpec((B, tl, GLA_COLS_PAD), lambda l: (0, l, 0)),
                  pl.BlockSpec(wup.shape, fixed), pl.BlockSpec(bup.shape, fixed),
                  pl.BlockSpec(tri.shape, fixed), pl.BlockSpec(sege.shape, fixed),
                  pl.BlockSpec(segv.shape, fixed), pl.BlockSpec(ng.shape, fixed)],
        out_specs=pl.BlockSpec((B, tl, W_GROUP), lambda l: (0, l, 0)),
        out_shape=jax.ShapeDtypeStruct((B, L, W_GROUP), ACT),
        scratch_shapes=[pltpu.VMEM((B, 256, 128), F32), pltpu.VMEM((B, tl, 128), F32),
                        pltpu.VMEM((B, tl, 256), F32)],
        compiler_params=_cparams("arbitrary"),
        name="gla_mixer",
    )(p.reshape(B, L, GLA_COLS_PAD), wup, bup, tri, sege, segv, ng)


def _gla_params(w_up, b_up, norm_g):
    wup = jnp.zeros((128, 128), F32).at[:GLA_RANK, :].set(w_up).astype(BF16)
    return wup, b_up.reshape(1, 128), norm_g.reshape(1, W_GROUP)


def _rw_chunk(p, st_ref, prev_ref, o_ref, bi, r0, prm):
    (mu, w0, w2, a0, a2, g2, kkg, ka, rk, lng, lnb, seg, tri) = prm
    C = RW_CHUNK
    W = W_GROUP
    prev_row = jnp.broadcast_to(prev_ref[bi, 0:1, :], p.shape)
    p_prev = _shift_rows(p, 1, prev_row)
    prev_ref[bi, 0:1, :] = p[C - 1:C, :]
    xm = p + (p_prev - p) * mu
    r = xm[:, 0:W]
    k = xm[:, W:2 * W]
    v = xm[:, 2 * W:3 * W]
    z = xm[:, 3 * W:3 * W + 128]

    w = -_softplus(-(w0 + _dg(jnp.tanh(z).astype(BF16), w2))) - 0.5
    logd = -jnp.exp(w)
    alr = _sigmoid(a0 + _dg(z.astype(BF16), a2))
    g = _dg(_sigmoid(z).astype(BF16), g2)
    lp = _dot_exact_lhs(tri, logd)
    yield

    kk = k * kkg
    kk = kk / jnp.maximum(jnp.sqrt(_dot1(kk * kk, seg)), 1e-12)
    k2 = k * (1.0 + (alr - 1.0) * ka)
    av = -kk
    bv = kk * alr
    bonus = _dot1(r * k2 * rk, seg) * v
    yield

    lp_c = lp[C - 1:C, :]
    e_neg = jnp.exp(-lp)
    e_rem = jnp.exp(lp_c - lp)
    ah = av * jnp.exp(lp - logd)
    rh = r * jnp.exp(lp)
    bh = bv * e_neg
    kh = k2 * e_neg
    b2 = bv * e_rem
    kc = k2 * e_rem
    p_c = jnp.exp(lp_c)

    lane_head = _iota((C, W), 1) >> 6
    hm = [lane_head == h for h in range(RW_HEADS)]
    zeros = jnp.zeros((C, W), F32)
    lhs = jnp.concatenate([jnp.where(m, ah, zeros) for m in hm] + [jnp.where(m, rh, zeros) for m in hm], axis=0)
    lhs = lhs.astype(BF16)
    xb = _dot1(lhs, bh, _NT).reshape(8, C, C)
    xk = _dot1(lhs, kh, _NT).reshape(8, C, C)
    yield

    ti = _iota((RW_HEADS, C, C), 1)
    tj = _iota((RW_HEADS, C, C), 2)
    strict = ti > tj
    incl = ti >= tj
    lm = jnp.where(strict, xb[0:4], 0.0)
    mak = jnp.where(strict, xk[0:4], 0.0)
    nrb = jnp.where(incl, xb[4:8], 0.0)
    nrk = jnp.where(incl, xk[4:8], 0.0)

    def bmm(a, b):
        return _dot1(a, b, _BNN)

    tinv = jnp.where(ti == tj, 1.0, 0.0) + lm
    lpow = lm
    for _ in range(5):
        lpow = bmm(lpow, lpow)
        yield
        tinv = tinv + bmm(tinv, lpow)
        yield

    def apply(m, x):
        full = _dot1(m.reshape(RW_HEADS * C, C), x).reshape(RW_HEADS, C, W)
        out = jnp.where(hm[0], full[0], zeros)
        for h in range(1, RW_HEADS):
            out = out + jnp.where(hm[h], full[h], zeros)
        return out

    tm = bmm(tinv, mak)
    a2h = apply(tinv, ah)
    nkv = apply(nrk, v)
    yield
    wv = apply(tm, v)
    r2 = rh + apply(nrb, a2h)
    bd = (_iota((W, W), 0) >> 6) == (_iota((W, W), 1) >> 6)
    eye = _iota((W, W), 0) == _iota((W, W), 1)
    a_c = jnp.where(eye, jnp.broadcast_to(p_c, (W, W)), 0.0) + jnp.where(bd, _dot1(b2, a2h, _TN), 0.0)
    yield
    y0 = apply(nrb, wv) + nkv
    g0 = jnp.where(bd, _dot1(b2, wv, _TN) + _dot1(kc, v, _TN), 0.0)
    s0 = st_ref[bi].astype(BF16)
    rs = _dot1(r2, s0)
    st_new = _dot1(a_c, s0)
    yield
    y = rs + y0
    st_ref[bi] = st_new + g0
    mean = _dot1(y, seg) * (1.0 / RW_N)
    yield
    yc = y - mean
    var = _dot1(yc * yc, seg) * (1.0 / RW_N)
    yield
    yn = yc * lax.rsqrt(var + RW_LN_EPS) * lng + lnb
    o_ref[bi, r0:r0 + C, :] = ((yn + bonus) * g).astype(o_ref.dtype)


def _rw_kernel(p_ref, mu_ref, w0_ref, w2_ref, a0_ref, a2_ref, g2_ref, kk_ref, ka_ref, rk_ref,
               lng_ref, lnb_ref, seg_ref, tri_ref, o_ref, st_ref, prev_ref):
    @pl.when(pl.program_id(0) == 0)
    def _():
        st_ref[...] = jnp.zeros_like(st_ref)
        prev_ref[...] = jnp.zeros_like(prev_ref)

    prm = tuple(ref[...] for ref in (mu_ref, w0_ref, w2_ref, a0_ref, a2_ref, g2_ref, kk_ref, ka_ref, rk_ref,
                                     lng_ref, lnb_ref, seg_ref, tri_ref))
    for r0 in range(0, p_ref.shape[1], RW_CHUNK):
        _round_robin([_rw_chunk(p_ref[bi, r0:r0 + RW_CHUNK, :].astype(F32), st_ref, prev_ref, o_ref, bi, r0, prm)
                      for bi in range(p_ref.shape[0])])


def _rw_mixer(p, prm, B, L, chunks_per_step=2):
    C = RW_CHUNK
    tl = C * chunks_per_step
    seg = (jnp.arange(256)[:, None] // RW_N == jnp.arange(256)[None, :] // RW_N).astype(BF16)
    tri = (jnp.arange(C)[None, :] <= jnp.arange(C)[:, None]).astype(BF16)
    fixed = lambda l: (0, 0)
    consts = tuple(prm) + (seg, tri)
    return pl.pallas_call(
        _rw_kernel,
        grid=(L // tl,),
        in_specs=[pl.BlockSpec((B, tl, RW_COLS), lambda l: (0, l, 0))]
        + [pl.BlockSpec(c.shape, fixed) for c in consts],
        out_specs=pl.BlockSpec((B, tl, W_GROUP), lambda l: (0, l, 0)),
        out_shape=jax.ShapeDtypeStruct((B, L, W_GROUP), ACT),
        scratch_shapes=[pltpu.VMEM((B, W_GROUP, W_GROUP), F32), pltpu.VMEM((B, 8, RW_COLS), F32)],
        compiler_params=_cparams("arbitrary"),
        name="rwkv7_mixer",
    )(p.reshape(B, L, RW_COLS), *consts)


def _rw_params(mu, w0, w2, a0, a2, g2, k_k, k_a, r_k, ln_g, ln_b):
    row = lambda t: t.reshape(1, -1)
    w2p = jnp.zeros((128, W_GROUP), F32).at[0:RW_W_RANK].set(w2).astype(BF16)
    a2p = jnp.zeros((128, W_GROUP), F32).at[RW_W_RANK:RW_W_RANK + RW_A_RANK].set(a2).astype(BF16)
    g2p = jnp.zeros((128, W_GROUP), F32).at[RW_W_RANK + RW_A_RANK:].set(g2).astype(BF16)
    return (row(mu), row(w0), w2p, row(a0), a2p, g2p, row(k_k), row(k_a), row(r_k), row(ln_g), row(ln_b))


def _conv_kernel(p_ref, w_ref, b_ref, lg_ref, lb_ref, o_ref, u_ref, s_ref):
    tl = p_ref.shape[0]
    H = CONV_HALO
    RC = 64
    n = tl + H - SUBLANES

    @pl.when(pl.program_id(1) == 0)
    def _():
        u_ref[0:H, :] = jnp.zeros((H, W_GROUP), F32)

    u_ref[H:H + tl, :] = p_ref[:, 0:W_GROUP].astype(F32) * _sigmoid(p_ref[:, W_GROUP:2 * W_GROUP].astype(F32))
    for b in range(1, SUBLANES):
        for r0 in range(0, n, RC):
            rows = min(RC, n - r0)
            s_ref[b, r0:r0 + rows, :] = u_ref[r0 + b:r0 + b + rows, :]
    off = H - (CONV_WIDTH - 1)
    for c in range(tl // RC):
        acc = jnp.zeros((RC, W_GROUP), F32)
        for j in range(CONV_WIDTH):
            a, b = divmod(off + j, SUBLANES)
            lo = c * RC + a * SUBLANES
            tap = u_ref[lo:lo + RC, :] if b == 0 else s_ref[b, lo:lo + RC, :]
            acc = acc + w_ref[j:j + 1, :] * tap
        y = acc + b_ref[...]
        mu = jnp.mean(y, axis=-1, keepdims=True)
        yc = y - mu
        yn = yc * lax.rsqrt(jnp.mean(yc * yc, axis=-1, keepdims=True) + CONV_LN_EPS) * lg_ref[...] + lb_ref[...]
        o_ref[c * RC:(c + 1) * RC, :] = (yn * _sigmoid(yn)).astype(o_ref.dtype)
    tail = u_ref[tl:tl + H, :]
    u_ref[0:H, :] = tail


def _conv_mixer(p, prm, B, L, tl=512):
    w, b, lg, lb = prm
    fixed = lambda b_, l: (0, 0)
    return pl.pallas_call(
        _conv_kernel,
        grid=(B, L // tl),
        in_specs=[pl.BlockSpec((None, tl, 2 * W_GROUP), lambda b_, l: (b_, l, 0)),
                  pl.BlockSpec(w.shape, fixed), pl.BlockSpec(b.shape, fixed),
                  pl.BlockSpec(lg.shape, fixed), pl.BlockSpec(lb.shape, fixed)],
        out_specs=pl.BlockSpec((None, tl, W_GROUP), lambda b_, l: (b_, l, 0)),
        out_shape=jax.ShapeDtypeStruct((B, L, W_GROUP), ACT),
        scratch_shapes=[pltpu.VMEM((tl + CONV_HALO, W_GROUP), F32),
                        pltpu.VMEM((SUBLANES, tl + CONV_HALO - SUBLANES, W_GROUP), F32)],
        compiler_params=_cparams("parallel", "arbitrary"),
        name="conv_mixer",
    )(p.reshape(B, L, 2 * W_GROUP), w, b, lg, lb)


def _xattn_body(x, g_ref, wq_ref, k_ref, v_ref, wo_ref):
    q = _dg(_rms(x, g_ref[...]).astype(BF16), wq_ref[...]).astype(BF16)
    heads = [None] * X_HEADS

    def head(hd):
        lo = hd * X_HEAD_DIM
        s = _dg(q[:, lo:lo + X_HEAD_DIM], k_ref[:, lo:lo + X_HEAD_DIM], _NT) * (X_HEAD_DIM ** -0.5)
        yield
        e = jnp.exp(s - jnp.max(s, axis=-1, keepdims=True))
        pr = e / jnp.sum(e, axis=-1, keepdims=True)
        heads[hd] = _dg(pr.astype(BF16), v_ref[:, lo:lo + X_HEAD_DIM])

    _round_robin([head(hd) for hd in range(X_HEADS)])
    o = jnp.concatenate(heads, axis=1).astype(BF16)
    return x + _dg(o, wo_ref[...])


def _router_body(h, g_ref, wr_ref, br_ref, xn_ref, eid_ref, gate_ref):
    xn = _rms(h, g_ref[...])
    xn_ref[...] = _pack_rows(xn)
    lg = _dot1(wr_ref[...], xn, _NT) + br_ref[...]
    tm = lg.shape[1]
    gl = lg[0:8, :]
    sub = _iota((8, tm), 0)
    gmax = jnp.max(gl, axis=0, keepdims=True)
    gsel = jnp.min(jnp.where(gl == gmax, sub, 8), axis=0, keepdims=True)
    g_w = 1.0 / jnp.sum(jnp.exp(gl - gmax), axis=0, keepdims=True)
    e_in = jnp.zeros((8, tm), F32)
    for grp in range(N_GROUPS):
        e_in = e_in + jnp.where(gsel == grp, lg[8 + 8 * grp:16 + 8 * grp, :], 0.0)
    t1 = jnp.max(e_in, axis=0, keepdims=True)
    i1 = jnp.min(jnp.where(e_in == t1, sub, 8), axis=0, keepdims=True)
    rest = jnp.where(sub == i1, -jnp.inf, e_in)
    t2 = jnp.max(rest, axis=0, keepdims=True)
    i2 = jnp.min(jnp.where(rest == t2, sub, 8), axis=0, keepdims=True)
    e2 = jnp.exp(t2 - t1)
    p1 = 1.0 / (1.0 + e2)
    eid_ref[0:1, :] = gsel * EXP_PER_GROUP + i1
    eid_ref[1:2, :] = gsel * EXP_PER_GROUP + i2
    gate_ref[0:1, :] = p1 * g_w
    gate_ref[1:2, :] = (e2 * p1) * g_w


def _post_mix_kernel(y5_ref, yg_ref, yr_ref, yc_ref, beta_ref, wout_ref, h_ref, gx_ref, wq_ref, k_ref, v_ref,
                     wo_ref, gf_ref, wr_ref, br_ref, h2_ref, xn_ref, eid_ref, gate_ref):
    h1 = _out_proj_body((y5_ref, yg_ref, yr_ref, yc_ref), beta_ref, wout_ref, h_ref[...])
    h2 = _xattn_body(h1, gx_ref, wq_ref, k_ref, v_ref, wo_ref)
    h2_ref[...] = h2
    _router_body(h2, gf_ref, wr_ref, br_ref, xn_ref, eid_ref, gate_ref)


def _post_mix(ys, beta, w_out, h, gx, wq, k, v, wo, gf, wr, br, B, L, n_mem, tm=ROUTE_TILE):
    T = B * L
    nl = L // tm
    tile = lambda b, l: (b * nl + l, 0)
    tile3 = lambda b, l: (b * nl + l, 0, 0)
    fixed = lambda b, l: (0, 0)
    mem = lambda b, l: (b, 0, 0)
    mat = pl.BlockSpec((D_MODEL, D_MODEL), fixed)
    vec = pl.BlockSpec((1, D_MODEL), fixed)
    return pl.pallas_call(
        _post_mix_kernel,
        grid=(B, nl),
        in_specs=[pl.BlockSpec((tm, W_GROUP), tile)] * 4
        + [vec, mat, pl.BlockSpec((tm, D_MODEL), tile), vec, mat,
           pl.BlockSpec((None, n_mem, D_MODEL), mem), pl.BlockSpec((None, n_mem, D_MODEL), mem), mat, vec,
           pl.BlockSpec(wr.shape, fixed), pl.BlockSpec(br.shape, fixed)],
        out_specs=[pl.BlockSpec((tm, D_MODEL), tile), pl.BlockSpec((tm, D_PACK), tile),
                   pl.BlockSpec((None, 2, tm), tile3), pl.BlockSpec((None, 2, tm), tile3)],
        out_shape=[jax.ShapeDtypeStruct((T, D_MODEL), F32), jax.ShapeDtypeStruct((T, D_PACK), U32),
                   jax.ShapeDtypeStruct((T // tm, 2, tm), jnp.int32),
                   jax.ShapeDtypeStruct((T // tm, 2, tm), F32)],
        compiler_params=_cparams("parallel", "parallel"),
        name="post_mix",
    )(*ys, beta, w_out, h, gx, wq, k.reshape(B, n_mem, D_MODEL), v.reshape(B, n_mem, D_MODEL), wo, gf, wr, br)


def _rank_kernel(eid_ref, ut_ref, rank_ref, cnt_ref, carry_ref):
    tm = eid_ref.shape[1]

    @pl.when(pl.program_id(0) == 0)
    def _():
        carry_ref[...] = jnp.zeros_like(carry_ref)

    ex = _iota((N_EXPERTS, tm), 0)
    oh0 = jnp.where(ex == eid_ref[0:1, :], 1.0, 0.0)
    oh1 = jnp.where(ex == eid_ref[1:2, :], 1.0, 0.0)
    oh = oh0 + oh1
    cum = _dg(oh.astype(BF16), ut_ref[...])
    carry = carry_ref[:, 0:1]
    before = cum - oh + carry
    rank_ref[0:1, :] = jnp.sum(oh0 * before, axis=0, keepdims=True).astype(jnp.int32)
    rank_ref[1:2, :] = jnp.sum(oh1 * before, axis=0, keepdims=True).astype(jnp.int32)
    total = carry + cum[:, tm - 1:tm]
    carry_ref[...] = jnp.broadcast_to(total, carry_ref.shape)
    cnt_ref[...] = jnp.broadcast_to(total, cnt_ref.shape).astype(jnp.int32)


def _rank(eid, tm=ROUTE_TILE):
    nt = eid.shape[0]
    ut = (jnp.arange(tm)[:, None] <= jnp.arange(tm)[None, :]).astype(BF16)
    return pl.pallas_call(
        _rank_kernel,
        grid=(nt,),
        in_specs=[pl.BlockSpec((None, 2, tm), lambda i: (i, 0, 0)), pl.BlockSpec((tm, tm), lambda i: (0, 0))],
        out_specs=[pl.BlockSpec((None, 2, tm), lambda i: (i, 0, 0)),
                   pl.BlockSpec((N_EXPERTS, 128), lambda i: (0, 0))],
        out_shape=[jax.ShapeDtypeStruct((nt, 2, tm), jnp.int32),
                   jax.ShapeDtypeStruct((N_EXPERTS, 128), jnp.int32)],
        scratch_shapes=[pltpu.VMEM((N_EXPERTS, 128), F32)],
        compiler_params=_cparams("arbitrary"),
        name="moe_rank",
    )(eid, ut)


def _slot_kernel(ps_ref, eid_ref, rank_ref, slot_ref):
    eid = eid_ref[...]
    acc = rank_ref[...]
    for e in range(N_EXPERTS):
        acc = acc + jnp.where(eid == e, ps_ref[e], 0)
    slot_ref[...] = acc


def _slots(pad_starts, eid, rank):
    nt, _, tm = eid.shape
    shape2d = (nt * 2, tm)
    out = pl.pallas_call(
        _slot_kernel,
        grid_spec=pltpu.PrefetchScalarGridSpec(
            num_scalar_prefetch=1,
            grid=(1,),
            in_specs=[pl.BlockSpec(shape2d, lambda i, ps: (0, 0))] * 2,
            out_specs=pl.BlockSpec(shape2d, lambda i, ps: (0, 0)),
        ),
        out_shape=jax.ShapeDtypeStruct(shape2d, jnp.int32),
        compiler_params=_cparams("arbitrary"),
        name="moe_slots",
    )(pad_starts, eid.reshape(shape2d), rank.reshape(shape2d))
    return out.reshape(nt, 2, tm).transpose(0, 2, 1).reshape(nt * tm * 2)


def _dispatch_kernel(lo_ref, np_ref, nu_ref, slot_ref, x_ref, xs_ref, z_ref, sem):
    groups = x_ref.shape[0]
    n_blocks = xs_ref.shape[0] // MOE_BLOCK

    def zero_block(blk):
        rows = pl.ds(pl.multiple_of(blk * MOE_BLOCK, MOE_BLOCK), MOE_BLOCK)
        return pltpu.make_async_copy(z_ref, xs_ref.at[rows], sem)

    def zero_fill(act):
        def per_expert(e, c):
            @pl.when(np_ref[e] > 0)
            def _():
                act(zero_block(lo_ref[e] // MOE_BLOCK))
            return c

        def tail(blk, c):
            act(zero_block(blk))
            return c

        lax.fori_loop(0, N_EXPERTS, per_expert, 0)
        lax.fori_loop(nu_ref[0], n_blocks, tail, 0)

    @pl.when(pl.program_id(0) == 0)
    def _():
        z_ref[...] = jnp.zeros_like(z_ref)
        zero_fill(lambda cp: cp.start())
        zero_fill(lambda cp: cp.wait())

    def issue(i, _):
        for u in range(SUBLANES):
            for k in range(2):
                s = slot_ref[i * (2 * SUBLANES) + (2 * u + k)]
                pltpu.make_async_copy(x_ref.at[i, pl.ds(u, 1)], xs_ref.at[pl.ds(s, 1)], sem).start(priority=k)
        return 0

    lax.fori_loop(0, groups, issue, 0)
    for _ in range(2):
        pltpu.make_async_copy(x_ref, x_ref, sem).wait()


def _dispatch(pad_lo, n_pad, n_used, slot, xn, n_slots, tm=ROUTE_TILE):
    T = xn.shape[0]
    g = tm // SUBLANES
    return pl.pallas_call(
        _dispatch_kernel,
        grid_spec=pltpu.PrefetchScalarGridSpec(
            num_scalar_prefetch=3,
            grid=(T // tm,),
            in_specs=[pl.BlockSpec((2 * tm,), lambda i, lo, npad, nu: (i,), memory_space=pltpu.SMEM),
                      pl.BlockSpec((g, SUBLANES, D_PACK), lambda i, lo, npad, nu: (i, 0, 0))],
            out_specs=pl.BlockSpec(memory_space=pl.ANY),
            scratch_shapes=[pltpu.VMEM((MOE_BLOCK, D_PACK), U32), pltpu.SemaphoreType.DMA(())],
        ),
        out_shape=jax.ShapeDtypeStruct((n_slots, D_PACK), U32),
        compiler_params=_cparams("arbitrary"),
        name="moe_dispatch",
    )(pad_lo, n_pad, n_used, slot, xn.reshape(T // SUBLANES, SUBLANES, D_PACK))


def _expert_kernel(be_ref, nu_ref, x_ref, wg_ref, wu_ref, wd_ref, y_ref, wgb_ref, wub_ref, wdb_ref):
    i = pl.program_id(0)
    used = i < nu_ref[0]

    @pl.when(jnp.logical_and(used, jnp.logical_or(i == 0, be_ref[i] != be_ref[jnp.maximum(i - 1, 0)])))
    def _():
        wgb_ref[...] = wg_ref[...].astype(BF16)
        wub_ref[...] = wu_ref[...].astype(BF16)
        wdb_ref[...] = wd_ref[...].astype(BF16)

    @pl.when(used)
    def _():
        xb = _unpack_rows(x_ref[...]).astype(BF16)
        gate = _dg(xb, wgb_ref[...])
        hid = (gate * _sigmoid(gate)) * _dg(xb, wub_ref[...])
        y_ref[...] = _pack_rows(_dg(hid.astype(BF16), wdb_ref[...]))

    @pl.when(jnp.logical_not(used))
    def _():
        y_ref[...] = jnp.zeros_like(y_ref)


def _experts(block_exp, n_used, xs, w_gate, w_up, w_down, layer):
    n_slots = xs.shape[0]
    nb = n_slots // MOE_BLOCK
    last = lambda i, nu: jnp.minimum(i, nu[0] - 1)
    wmap = lambda i, be, nu: (layer, be[last(i, nu)], 0, 0)
    return pl.pallas_call(
        _expert_kernel,
        grid_spec=pltpu.PrefetchScalarGridSpec(
            num_scalar_prefetch=2,
            grid=(nb,),
            in_specs=[pl.BlockSpec((MOE_BLOCK, D_PACK), lambda i, be, nu: (last(i, nu), 0)),
                      pl.BlockSpec((None, None, D_MODEL, D_EXPERT), wmap),
                      pl.BlockSpec((None, None, D_MODEL, D_EXPERT), wmap),
                      pl.BlockSpec((None, None, D_EXPERT, D_MODEL), wmap)],
            out_specs=pl.BlockSpec((MOE_BLOCK, D_PACK), lambda i, be, nu: (i, 0)),
            scratch_shapes=[pltpu.VMEM((D_MODEL, D_EXPERT), BF16), pltpu.VMEM((D_MODEL, D_EXPERT), BF16),
                            pltpu.VMEM((D_EXPERT, D_MODEL), BF16)],
        ),
        out_shape=jax.ShapeDtypeStruct((n_slots, D_PACK), U32),
        compiler_params=_cparams("arbitrary"),
        name="moe_experts",
    )(block_exp, n_used, xs, w_gate, w_up, w_down)


def _combine_kernel(slot_ref, gate_ref, h_ref, ys_ref, o_ref, y0_ref, y1_ref, sem):
    tm = h_ref.shape[0]
    groups = tm // SUBLANES
    bufs = (y0_ref, y1_ref)

    def issue(i, _):
        for u in range(SUBLANES):
            for k in range(2):
                s = slot_ref[i * (2 * SUBLANES) + (2 * u + k)]
                pltpu.make_async_copy(ys_ref.at[pl.ds(s, 1)], bufs[k].at[i, pl.ds(u, 1)], sem).start(priority=k)
        return 0

    lax.fori_loop(0, groups, issue, 0)
    for k in range(2):
        pltpu.make_async_copy(bufs[k], bufs[k], sem).wait()
    y0 = _unpack_rows(y0_ref[...].reshape(tm, D_PACK))
    y1 = _unpack_rows(y1_ref[...].reshape(tm, D_PACK))
    o_ref[...] = h_ref[...] + gate_ref[:, 0:1] * y0 + gate_ref[:, 1:2] * y1


def _combine(slot, gate_t, h, ys, tm=ROUTE_TILE):
    T = h.shape[0]
    g = tm // SUBLANES
    return pl.pallas_call(
        _combine_kernel,
        grid=(T // tm,),
        in_specs=[pl.BlockSpec((2 * tm,), lambda i: (i,), memory_space=pltpu.SMEM),
                  pl.BlockSpec((tm, 2), lambda i: (i, 0)),
                  pl.BlockSpec((tm, D_MODEL), lambda i: (i, 0)),
                  pl.BlockSpec(memory_space=pl.ANY)],
        out_specs=pl.BlockSpec((tm, D_MODEL), lambda i: (i, 0)),
        scratch_shapes=[pltpu.VMEM((g, SUBLANES, D_PACK), U32), pltpu.VMEM((g, SUBLANES, D_PACK), U32),
                        pltpu.SemaphoreType.DMA(())],
        out_shape=jax.ShapeDtypeStruct((T, D_MODEL), F32),
        compiler_params=_cparams("arbitrary"),
        name="moe_combine",
    )(slot, gate_t, h, ys)


def _router_params(group_w, group_b, expert_w, expert_b):
    wr = jnp.zeros((ROUTE_ROWS, D_MODEL), F32).at[0:N_GROUPS].set(group_w.T).at[8:].set(expert_w.T)
    br = jnp.full((ROUTE_ROWS, 1), -1e30, F32).at[0:N_GROUPS, 0].set(group_b).at[8:, 0].set(expert_b)
    return wr, br


def _moe(h, xn, eid, gate, w_gate, w_up, w_down, layer):
    T = h.shape[0]
    rank, cnt = _rank(eid)
    counts = cnt[:, 0]
    padded = (counts + MOE_BLOCK - 1) // MOE_BLOCK * MOE_BLOCK
    pad_ends = jnp.cumsum(padded)
    pad_starts = (pad_ends - padded).astype(jnp.int32)
    n_slots = ((T * 2 + MOE_BLOCK - 1) // MOE_BLOCK + N_EXPERTS) * MOE_BLOCK
    block_start = jnp.arange(n_slots // MOE_BLOCK, dtype=jnp.int32) * MOE_BLOCK
    owner = jnp.sum((pad_ends[None, :] <= block_start[:, None]).astype(jnp.int32), axis=1)
    block_exp = jnp.minimum(owner, N_EXPERTS - 1).astype(jnp.int32)
    slot = _slots(pad_starts, eid, rank)
    n_used = (pad_ends[N_EXPERTS - 1:] // MOE_BLOCK).astype(jnp.int32)
    xs = _dispatch((pad_starts + counts).astype(jnp.int32), (padded - counts).astype(jnp.int32), n_used,
                   slot, xn, n_slots)
    ys = _experts(block_exp, n_used, xs, w_gate, w_up, w_down, layer)
    gate_t = gate.transpose(0, 2, 1).reshape(T, 2)
    return _combine(slot, gate_t, h, ys)


def kernel(x, mem, norm_mix_g, w_in, w_out, mix_beta, s5_lam_re, s5_lam_im, s5_b_re, s5_b_im, s5_c_re, s5_c_im, s5_d, s5_log_dt, s5_glu_w, s5_glu_b, gla_w_up, gla_b_up, gla_norm_g, rw_mu, rw_w0, rw_w2, rw_a0, rw_a2, rw_g2, rw_k_k, rw_k_a, rw_r_k, rw_ln_g, rw_ln_b, conv_w, conv_b, conv_ln_g, conv_ln_b, norm_xattn_g, norm_mem_g, xa_wq, xa_wk, xa_wv, xa_wo, norm_ffn_g, moe_group_w, moe_group_b, moe_expert_w, moe_expert_b, moe_w_gate, moe_w_up, moe_w_down, norm_final_g):
    B, L, D = x.shape
    n_mem = mem.shape[1]
    depth = w_in.shape[0]
    T = B * L
    h = x.reshape(T, D)
    mem2d = mem.reshape(B * n_mem, D)
    row = lambda t: t.reshape(1, -1)
    c0 = W_GROUP
    c1 = c0 + 784
    c2 = c1 + RW_COLS
    for l in range(depth):
        wl = w_in[l]
        w5 = wl[:, :c0].astype(BF16)
        wg = jnp.pad(wl[:, c0:c1], ((0, 0), (0, GLA_COLS_PAD - 784))).astype(BF16)
        wr = wl[:, c1:c2].astype(BF16)
        wc = wl[:, c2:].astype(BF16)
        p5, pg, pr, pc = _in_proj(h, row(norm_mix_g[l]), w5, wg, wr, wc)
        y5 = _s5_mixer(p5, _s5_params(s5_lam_re[l], s5_lam_im[l], s5_b_re[l], s5_b_im[l], s5_c_re[l], s5_c_im[l],
                                      s5_d[l], s5_log_dt[l], s5_glu_w[l], s5_glu_b[l]), B, L)
        yg = _gla_mixer(pg, _gla_params(gla_w_up[l], gla_b_up[l], gla_norm_g[l]), B, L)
        yr = _rw_mixer(pr, _rw_params(rw_mu[l], rw_w0[l], rw_w2[l], rw_a0[l], rw_a2[l], rw_g2[l], rw_k_k[l],
                                      rw_k_a[l], rw_r_k[l], rw_ln_g[l], rw_ln_b[l]), B, L)
        yc = _conv_mixer(pc, (conv_w[l], row(conv_b[l]), row(conv_ln_g[l]), row(conv_ln_b[l])), B, L)
        ys = [t.reshape(T, W_GROUP) for t in (y5, yg, yr, yc)]
        kmem, vmem = _kv_proj(mem2d, row(norm_mem_g[l]), xa_wk[l].astype(BF16), xa_wv[l].astype(BF16))
        wr, br = _router_params(moe_group_w[l], moe_group_b[l], moe_expert_w[l], moe_expert_b[l])
        h, xn, eid, gate = _post_mix(ys, row(mix_beta[l]), w_out[l].astype(BF16), h, row(norm_xattn_g[l]),
                                     xa_wq[l].astype(BF16), kmem, vmem, xa_wo[l].astype(BF16),
                                     row(norm_ffn_g[l]), wr, br, B, L, n_mem)
        h = _moe(h, xn, eid, gate, moe_w_gate, moe_w_up, moe_w_down, l)
    return _final_norm(h, row(norm_final_g)).reshape(B, L, D)
```
